```python
import math, functools
import jax, jax.numpy as jnp
from jax import lax
import numpy as np

D_MODEL = 4096
BATCH = 4
SEQ = 2048
DEPTH = 1
DEC_BATCH = 32
DEC_SEQ = 1
PAST_LEN = 8192
PAGE_SIZE = 128

HEAD_DIM = 64
H_ATTN = 3 * (D_MODEL // 512)
ATTN_WIDTH = H_ATTN * HEAD_DIM
H_RWKV = D_MODEL // HEAD_DIM - H_ATTN
RWKV_WIDTH = H_RWKV * HEAD_DIM
MIX_WIDTH = ATTN_WIDTH + RWKV_WIDTH
DILATIONS = ((128, 1), (512, 4), (2048, 16))
MAX_WINDOW = 2048
NUM_BUCKETS = 32
MAX_DISTANCE = 2048
DECAY_LORA = 128
AAA_LORA = 128
GATE_LORA = 480
ATTN_COLS = 3 * ATTN_WIDTH
RWKV_COLS = 3 * RWKV_WIDTH + DECAY_LORA + AAA_LORA + GATE_LORA
IN_COLS = ATTN_COLS + RWKV_COLS
RWKV_SPLITS = (RWKV_WIDTH, 2 * RWKV_WIDTH, 3 * RWKV_WIDTH,
               3 * RWKV_WIDTH + DECAY_LORA, 3 * RWKV_WIDTH + DECAY_LORA + AAA_LORA)
N_EXPERTS = 256
TOP_K = 8
N_GROUPS = 8
TOPK_GROUPS = 4
EXPERT_FF = 1024
SHARED_FF = 1024
ROUTED_SCALE = 2.5
MOE_BLOCK = 64
MOE_BLOCK_SMALL = 8
RMS_EPS = 1e-6
GN_EPS = 64e-5
ATTN_SCALE = HEAD_DIM ** -0.5

kernel_name = 'hymba_dilated_rwkv7_moe_step'


def rms_norm(x, g):
    xf = x.astype(jnp.float32)
    y = xf * lax.rsqrt(jnp.mean(xf * xf, axis=-1, keepdims=True) + RMS_EPS)
    return (y * g).astype(x.dtype)


def rel_bucket(dist):
    n_exact = NUM_BUCKETS // 2
    df = jnp.maximum(dist, 1).astype(jnp.float32)
    large = n_exact + (jnp.log(df / n_exact) / math.log(MAX_DISTANCE / n_exact)
                       * (NUM_BUCKETS - n_exact)).astype(jnp.int32)
    return jnp.where(dist < n_exact, dist, jnp.minimum(large, NUM_BUCKETS - 1))


def dilated_band_pattern(q, k, v, rel_bias, window, dilation):
    B, S, H, E = q.shape
    n = window // dilation
    L = S // dilation
    nb = -(-L // n)
    Lp = nb * n

    def to_blocks(t):
        t = t.reshape(B, L, dilation, H, E).transpose(0, 2, 1, 3, 4)
        t = jnp.pad(t, ((0, 0), (0, 0), (0, Lp - L), (0, 0), (0, 0)))
        return t.reshape(B, dilation, nb, n, H, E)

    def with_prev(t):
        prev = jnp.pad(t, ((0, 0), (0, 0), (1, 0), (0, 0), (0, 0), (0, 0)))[:, :, :-1]
        return jnp.concatenate([prev, t], axis=3)

    qb = to_blocks(q)
    kb = with_prev(to_blocks(k))
    vb = with_prev(to_blocks(v))
    qi = jnp.arange(n)[:, None]
    ki = jnp.arange(2 * n)[None, :]
    j = qi + n - ki
    band = (j >= 0) & (j <= n)
    bias = rel_bias[rel_bucket(jnp.clip(j, 0, n) * dilation)].transpose(2, 0, 1)
    key_l = jnp.arange(nb)[:, None, None] * n + ki[None] - n
    mask = band[None] & (key_l >= 0)
    s = jnp.einsum('brcqhe,brckhe->brchqk', qb, kb) * ATTN_SCALE + bias
    s = jnp.where(mask[None, None, :, None], s, -jnp.inf)
    m = jnp.max(s, axis=-1)
    p = jnp.exp(s - m[..., None])
    den = jnp.sum(p, axis=-1)
    o = jnp.einsum('brchqk,brckhe->brcqhe', p, vb)

    def from_blocks(t):
        t = t.reshape((B, dilation, Lp) + t.shape[4:])[:, :, :L]
        t = jnp.swapaxes(t, 1, 2)
        return t.reshape((B, S) + t.shape[3:])

    return (from_blocks(o), from_blocks(jnp.swapaxes(m, 3, 4)),
            from_blocks(jnp.swapaxes(den, 3, 4)))


def prompt_dilated_attention(q, k, v, rel_bias):
    q, k, v = (t.astype(jnp.float32) for t in (q, k, v))
    rb = rel_bias.astype(jnp.float32)
    outs = [dilated_band_pattern(q, k, v, rb, w, d) for (w, d) in DILATIONS]
    m_all = jnp.max(jnp.stack([m for _, m, _ in outs]), axis=0)
    num = sum(jnp.exp(m - m_all)[..., None] * o for o, m, _ in outs)
    den = sum(jnp.exp(m - m_all) * s for _, m, s in outs)
    return num / den[..., None]


def sample_dilated_attention(q, k_new, v_new, k_buf, v_buf, rel_bias):
    Bd, T, H, E = q.shape
    W = k_buf.shape[1]
    offsets = np.concatenate([d * np.arange(w // d + 1) for (w, d) in DILATIONS]).astype(np.int32)
    idx = W + np.arange(T)[:, None] - offsets[None, :]
    valid = jnp.asarray(idx >= 0)
    idx = jnp.asarray(np.maximum(idx, 0).astype(np.int32))
    kcat = jnp.concatenate([k_buf, k_new], axis=1).astype(jnp.float32)
    vcat = jnp.concatenate([v_buf, v_new], axis=1).astype(jnp.float32)
    kg = kcat[:, idx]
    vg = vcat[:, idx]
    bias = rel_bias.astype(jnp.float32)[rel_bucket(jnp.asarray(offsets))].T
    s = jnp.einsum('bthe,btjhe->bthj', q.astype(jnp.float32), kg) * ATTN_SCALE + bias
    s = jnp.where(valid[None, :, None, :], s, -jnp.inf)
    p = jax.nn.softmax(s, axis=-1)
    return jnp.einsum('bthj,btjhe->bthe', p, vg)


def rwkv_time_mix(p, shift0, state0, tmix_mu, w0, w2, a0, a2, g2, k_k, k_a, r_k, ln_w, ln_b):
    B, T, _ = p.shape
    pf = p.astype(jnp.float32)
    shifted = jnp.concatenate([shift0[:, None].astype(jnp.float32), pf[:, :-1]], axis=1)
    xm = pf + tmix_mu * (shifted - pf)
    r, k, v, lw, la, lg = jnp.split(xm, RWKV_SPLITS, axis=-1)
    w_log = -jax.nn.softplus(-(w0 + jnp.tanh(lw) @ w2)) - 0.5
    decay = jnp.exp(-jnp.exp(w_log))
    a = jax.nn.sigmoid(a0 + la @ a2)
    g = jax.nn.sigmoid(lg) @ g2
    heads = lambda t: t.reshape(B, T, H_RWKV, HEAD_DIM)
    kk = heads(k * k_k)
    kk = kk / jnp.maximum(jnp.linalg.norm(kk, axis=-1, keepdims=True), 1e-12)
    k = k * (1.0 + (a - 1.0) * k_a)
    r_h, w_h, k_h, v_h, a_h = heads(r), heads(decay), heads(k), heads(v), heads(a)

    def step(S, inp):
        r_t, w_t, k_t, v_t, kk_t, a_t = inp
        sa = jnp.einsum('bhvk,bhk->bhv', S, -kk_t)
        S = (S * w_t[:, :, None, :] + sa[..., None] * (kk_t * a_t)[:, :, None, :]
             + v_t[..., None] * k_t[:, :, None, :])
        return S, jnp.einsum('bhvk,bhk->bhv', S, r_t)

    xs = tuple(jnp.swapaxes(t, 0, 1) for t in (r_h, w_h, k_h, v_h, kk, a_h))
    S_fin, ys = lax.scan(step, state0.astype(jnp.float32), xs)
    y = jnp.swapaxes(ys, 0, 1)
    mu_y = jnp.mean(y, axis=-1, keepdims=True)
    var = jnp.mean(jnp.square(y - mu_y), axis=-1, keepdims=True)
    y = ((y - mu_y) * lax.rsqrt(var + GN_EPS)).reshape(B, T, RWKV_WIDTH) * ln_w + ln_b
    bonus = jnp.sum(r_h * k_h * r_k, axis=-1, keepdims=True) * v_h
    out = (y + bonus.reshape(B, T, RWKV_WIDTH)) * g
    return out.astype(p.dtype), S_fin.astype(state0.dtype), p[:, -1]


def moe_ffn(h, layer, router_w, router_b, w_gate, w_up, w_down, ws_gate, ws_up, ws_down):
    B, T, D = h.shape
    N = B * T
    xt = h.reshape(N, D)
    scores = jax.nn.sigmoid((xt @ router_w).astype(jnp.float32))
    choice = scores + router_b
    grp_score = jnp.sum(lax.top_k(choice.reshape(N, N_GROUPS, N_EXPERTS // N_GROUPS), 2)[0], axis=-1)
    _, gidx = lax.top_k(grp_score, TOPK_GROUPS)
    gmask = jnp.any(gidx[..., None] == jnp.arange(N_GROUPS), axis=1)
    choice = jnp.where(jnp.repeat(gmask, N_EXPERTS // N_GROUPS, axis=1), choice, -jnp.inf)
    _, eidx = lax.top_k(choice, TOP_K)
    sel = jnp.take_along_axis(scores, eidx, axis=1)
    gate = sel / jnp.sum(sel, axis=-1, keepdims=True) * ROUTED_SCALE

    block = MOE_BLOCK if N * TOP_K >= N_EXPERTS * MOE_BLOCK else MOE_BLOCK_SMALL
    n_blocks = -(-(N * TOP_K) // block) + N_EXPERTS
    n_rows = n_blocks * block
    flat_e = eidx.reshape(-1)
    flat_tok = jnp.repeat(jnp.arange(N, dtype=jnp.int32), TOP_K)
    order = jnp.argsort(flat_e)
    se, stok, sw = flat_e[order], flat_tok[order], gate.reshape(-1)[order]
    counts = jnp.zeros((N_EXPERTS,), jnp.int32).at[flat_e].add(1)
    padded = (counts + block - 1) // block * block
    pad_end = jnp.cumsum(padded)
    pad_start = pad_end - padded
    start = jnp.cumsum(counts) - counts
    dest = pad_start[se] + jnp.arange(N * TOP_K, dtype=jnp.int32) - start[se]
    row_tok = jnp.full((n_rows,), N, jnp.int32).at[dest].set(stok)
    row_w = jnp.zeros((n_rows,), jnp.float32).at[dest].set(sw)
    blk_start = jnp.arange(n_blocks, dtype=jnp.int32) * block
    blk_expert = jnp.minimum(jnp.searchsorted(pad_end, blk_start, side='right'), N_EXPERTS - 1)
    blk_used = blk_start < pad_end[-1]
    xpad = jnp.concatenate([xt, jnp.zeros((1, D), xt.dtype)], axis=0)
    xb = xpad[row_tok].reshape(n_blocks, block, D)

    def expert_block(args):
        xe, e, used = args
        def compute(xe):
            return (jax.nn.silu(xe @ w_gate[layer, e]) * (xe @ w_up[layer, e])) @ w_down[layer, e]
        return lax.cond(used, compute, jnp.zeros_like, xe)

    yb = lax.map(expert_block, (xb, blk_expert, blk_used))
    yr = yb.reshape(n_rows, D).astype(jnp.float32) * row_w[:, None]
    routed = jax.ops.segment_sum(yr, row_tok, num_segments=N + 1)[:N]
    shared = (jax.nn.silu(xt @ ws_gate) * (xt @ ws_up)) @ ws_down
    return (routed + shared.astype(jnp.float32)).astype(h.dtype).reshape(B, T, D)


def trunk_layer(x, c, attention, shift0, state0, layer,
                norm1, norm2, w_ada, b_ada, w_in, w_out,
                tmix_mu, w0, w2, a0, a2, g2, k_k, k_a, r_k, ln_w, ln_b,
                router_w, router_b, ws_gate, ws_up, ws_down, w_gate, w_up, w_down):
    B, T, _ = x.shape
    mod = jax.nn.silu(c) @ w_ada + b_ada
    sh1, sc1, gt1, sh2, sc2, gt2 = jnp.split(mod[:, None], 6, axis=-1)
    h = rms_norm(x, norm1) * (1.0 + sc1) + sh1
    proj = h @ w_in
    q, k, v = (t.reshape(B, T, H_ATTN, HEAD_DIM) for t in jnp.split(proj[..., :ATTN_COLS], 3, axis=-1))
    o_attn = attention(q, k, v).astype(x.dtype).reshape(B, T, ATTN_WIDTH)
    o_rwkv, state, shift = rwkv_time_mix(proj[..., ATTN_COLS:], shift0, state0, tmix_mu, w0, w2,
                                         a0, a2, g2, k_k, k_a, r_k, ln_w, ln_b)
    x = x + gt1 * (jnp.concatenate([o_attn, o_rwkv], axis=-1) @ w_out)
    h2 = rms_norm(x, norm2) * (1.0 + sc2) + sh2
    x = x + gt2 * moe_ffn(h2, layer, router_w, router_b, w_gate, w_up, w_down, ws_gate, ws_up, ws_down)
    return x, k, v, state, shift


def setup_inputs(seed: int = 0) -> dict:
    key = jax.random.key(seed)
    ks = iter(jax.random.split(key, 48))
    f32 = jnp.float32

    def nrm(shape, std):
        return jax.random.normal(next(ks), shape, f32) * std

    def unif(shape, lo, hi):
        return jax.random.uniform(next(ks), shape, f32, lo, hi)

    w_buf = min(MAX_WINDOW, PAST_LEN)
    L = DEPTH
    return {
        'x_prompt': nrm((BATCH, SEQ, D_MODEL), 1.0),
        'x_sample': nrm((DEC_BATCH, DEC_SEQ, D_MODEL), 1.0),
        'cache_attn_k': nrm((L, DEC_BATCH, w_buf, H_ATTN, HEAD_DIM), 1.0),
        'cache_attn_v': nrm((L, DEC_BATCH, w_buf, H_ATTN, HEAD_DIM), 1.0),
        'state_rwkv': nrm((L, DEC_BATCH, H_RWKV, HEAD_DIM, HEAD_DIM), 0.3),
        'state_shift': nrm((L, DEC_BATCH, RWKV_COLS), 1.0),
        'c_prompt': nrm((BATCH, D_MODEL), 1.0),
        'c_sample': nrm((DEC_BATCH, D_MODEL), 1.0),
        'rel_bias': nrm((NUM_BUCKETS, H_ATTN), 0.5),
        'norm1': 1.0 + nrm((L, D_MODEL), 0.02),
        'norm2': 1.0 + nrm((L, D_MODEL), 0.02),
        'final_norm': 1.0 + nrm((D_MODEL,), 0.02),
        'w_ada': nrm((L, D_MODEL, 6 * D_MODEL), 0.5 * D_MODEL ** -0.5),
        'b_ada': nrm((L, 6 * D_MODEL), 0.02),
        'w_in': nrm((L, D_MODEL, IN_COLS), D_MODEL ** -0.5),
        'w_out': nrm((L, MIX_WIDTH, D_MODEL), MIX_WIDTH ** -0.5),
        'tmix_mu': unif((L, RWKV_COLS), 0.0, 1.0),
        'w0': unif((L, RWKV_WIDTH), -4.0, 0.0),
        'w2': nrm((L, DECAY_LORA, RWKV_WIDTH), 0.5 * DECAY_LORA ** -0.5),
        'a0': nrm((L, RWKV_WIDTH), 0.5),
        'a2': nrm((L, AAA_LORA, RWKV_WIDTH), 0.5 * AAA_LORA ** -0.5),
        'g2': nrm((L, GATE_LORA, RWKV_WIDTH), GATE_LORA ** -0.5),
        'k_k': 0.85 + nrm((L, RWKV_WIDTH), 0.02),
        'k_a': 1.0 + nrm((L, RWKV_WIDTH), 0.02),
        'r_k': nrm((L, H_RWKV, HEAD_DIM), 0.1),
        'ln_w': 1.0 + nrm((L, RWKV_WIDTH), 0.02),
        'ln_b': nrm((L, RWKV_WIDTH), 0.02),
        'router_w': nrm((L, D_MODEL, N_EXPERTS), D_MODEL ** -0.5),
        'router_b': nrm((L, N_EXPERTS), 0.01),
        'ws_gate': nrm((L, D_MODEL, SHARED_FF), D_MODEL ** -0.5),
        'ws_up': nrm((L, D_MODEL, SHARED_FF), D_MODEL ** -0.5),
        'ws_down': nrm((L, SHARED_FF, D_MODEL), SHARED_FF ** -0.5),
        'w_gate': nrm((L, N_EXPERTS, D_MODEL, EXPERT_FF), D_MODEL ** -0.5),
        'w_up': nrm((L, N_EXPERTS, D_MODEL, EXPERT_FF), D_MODEL ** -0.5),
        'w_down': nrm((L, N_EXPERTS, EXPERT_FF, D_MODEL), EXPERT_FF ** -0.5),
    }


def reference(x_prompt, x_sample, cache_attn_k, cache_attn_v, state_rwkv, state_shift,
              c_prompt, c_sample, rel_bias, norm1, norm2, final_norm, w_ada, b_ada, w_in, w_out,
              tmix_mu, w0, w2, a0, a2, g2, k_k, k_a, r_k, ln_w, ln_b, router_w, router_b,
              ws_gate, ws_up, ws_down, w_gate, w_up, w_down):
    B, S, _ = x_prompt.shape
    n_keep = min(MAX_WINDOW, S)
    xp, xs = x_prompt, x_sample
    kp_l, vp_l, sp_l, hp_l, ks_l, vs_l, ss_l, hs_l = [], [], [], [], [], [], [], []
    for l in range(DEPTH):
        lp = (norm1[l], norm2[l], w_ada[l], b_ada[l], w_in[l], w_out[l],
              tmix_mu[l], w0[l], w2[l], a0[l], a2[l], g2[l], k_k[l], k_a[l], r_k[l], ln_w[l], ln_b[l],
              router_w[l], router_b[l], ws_gate[l], ws_up[l], ws_down[l], w_gate, w_up, w_down)
        attn_p = functools.partial(prompt_dilated_attention, rel_bias=rel_bias)
        shift0 = jnp.zeros((B, RWKV_COLS), xp.dtype)
        state0 = jnp.zeros((B, H_RWKV, HEAD_DIM, HEAD_DIM), xp.dtype)
        xp, kp, vp, sp, hp = trunk_layer(xp, c_prompt, attn_p, shift0, state0, l, *lp)
        kp_l.append(kp[:, -n_keep:])
        vp_l.append(vp[:, -n_keep:])
        sp_l.append(sp)
        hp_l.append(hp)
        attn_s = functools.partial(sample_dilated_attention, k_buf=cache_attn_k[l],
                                   v_buf=cache_attn_v[l], rel_bias=rel_bias)
        xs, ksm, vsm, ssm, hsm = trunk_layer(xs, c_sample, attn_s, state_shift[l], state_rwkv[l], l, *lp)
        ks_l.append(ksm)
        vs_l.append(vsm)
        ss_l.append(ssm)
        hs_l.append(hsm)
    y_prompt = rms_norm(xp, final_norm)
    y_sample = rms_norm(xs, final_norm)
    return (y_prompt, y_sample,
            jnp.stack(kp_l), jnp.stack(vp_l), jnp.stack(sp_l), jnp.stack(hp_l),
            jnp.stack(ks_l), jnp.stack(vs_l), jnp.stack(ss_l), jnp.stack(hs_l))
```

```python
import functools
import math

import numpy as np
import jax
import jax.numpy as jnp
from jax import lax
from jax.experimental import pallas as pl
from jax.experimental.pallas import tpu as pltpu

F32 = jnp.float32
BF16 = jnp.bfloat16
HIGHEST = lax.Precision.HIGHEST

D_MODEL = 4096
HEAD_DIM = 64
H_ATTN = 24
ATTN_WIDTH = H_ATTN * HEAD_DIM
H_RWKV = 40
RWKV_WIDTH = H_RWKV * HEAD_DIM
DILATIONS = ((128, 1), (512, 4), (2048, 16))
NUM_BUCKETS = 32
MAX_DISTANCE = 2048
DECAY_LORA = 128
AAA_LORA = 128
GATE_LORA = 480
LORA_WIDTH = DECAY_LORA + AAA_LORA + GATE_LORA
ATTN_COLS = 3 * ATTN_WIDTH
RWKV_MAIN = 3 * RWKV_WIDTH
RWKV_COLS = RWKV_MAIN + LORA_WIDTH
N_EXPERTS = 256
TOP_K = 8
N_GROUPS = 8
TOPK_GROUPS = 4
EXPERT_FF = 1024
ROUTED_SCALE = 2.5
RMS_EPS = 1e-6
GN_EPS = 64e-5
ATTN_SCALE = HEAD_DIM ** -0.5
NEG = -1e30

LANES = 128
MOE_ROWS = 128
ATTN_TILE = 256
MIB = 1 << 20


def _cparams(dims, vmem_mib=None):
    kw = dict(dimension_semantics=dims)
    if vmem_mib is not None:
        kw["vmem_limit_bytes"] = vmem_mib * MIB
    return pltpu.CompilerParams(**kw)


def _dot(a, b, precision=None):
    return lax.dot_general(a, b, (((a.ndim - 1,), (0,)), ((), ())),
                           precision=precision, preferred_element_type=F32)


def _ada_body(c_ref, w_ref, b_ref, o_ref):
    c = c_ref[...]
    o_ref[...] = _dot(c * jax.nn.sigmoid(c), w_ref[...]) + b_ref[...]


def _ada_mod(c_all, w_ada, b_ada, tn=512):
    m, k = c_all.shape
    n = w_ada.shape[1]
    return pl.pallas_call(
        _ada_body,
        grid=(n // tn,),
        in_specs=[pl.BlockSpec((m, k), lambda j: (0, 0)),
                  pl.BlockSpec((k, tn), lambda j: (0, j)),
                  pl.BlockSpec((1, tn), lambda j: (0, j))],
        out_specs=pl.BlockSpec((m, tn), lambda j: (0, j)),
        out_shape=jax.ShapeDtypeStruct((m, n), F32),
        compiler_params=_cparams(("arbitrary",), 40),
        name="ada_mod",
    )(c_all, w_ada, b_ada.reshape(1, n))


def _norm_mod_body(x_ref, g_ref, sc_ref, sh_ref, o_ref):
    x = x_ref[0]
    y = x * lax.rsqrt(jnp.mean(x * x, axis=-1, keepdims=True) + RMS_EPS) * g_ref[...]
    o_ref[0] = (y * (1.0 + sc_ref[0]) + sh_ref[0]).astype(o_ref.dtype)


def _norm_mod_router_body(x_ref, g_ref, sc_ref, sh_ref, rw_ref, o_ref, s_ref):
    x = x_ref[0]
    y = x * lax.rsqrt(jnp.mean(x * x, axis=-1, keepdims=True) + RMS_EPS) * g_ref[...]
    h = y * (1.0 + sc_ref[0]) + sh_ref[0]
    o_ref[0] = h.astype(o_ref.dtype)
    s_ref[0] = jax.nn.sigmoid(_dot(h, rw_ref[...], precision=HIGHEST))


def _mod_spec(mod3, chunk, tm):
    if mod3.shape[1] == 1:
        return pl.BlockSpec((1, 1, D_MODEL), lambda b, i: (b, 0, chunk))
    return pl.BlockSpec((1, tm, D_MODEL), lambda b, i: (b, i, chunk))


def _norm_mod(x3, g, mod3, sc_chunk, sh_chunk, tm, router_w=None):
    b, t, d = x3.shape
    in_specs = [pl.BlockSpec((1, tm, d), lambda b_, i: (b_, i, 0)),
                pl.BlockSpec((1, d), lambda b_, i: (0, 0)),
                _mod_spec(mod3, sc_chunk, tm),
                _mod_spec(mod3, sh_chunk, tm)]
    h_spec = pl.BlockSpec((1, tm, d), lambda b_, i: (b_, i, 0))
    h_shape = jax.ShapeDtypeStruct((b, t, d), BF16)
    if router_w is None:
        return pl.pallas_call(
            _norm_mod_body, grid=(b, t // tm), in_specs=in_specs, out_specs=h_spec, out_shape=h_shape,
            compiler_params=_cparams(("arbitrary", "arbitrary"), 40), name="norm_mod",
        )(x3, g.reshape(1, d), mod3, mod3)
    in_specs.append(pl.BlockSpec((d, N_EXPERTS), lambda b_, i: (0, 0)))
    return pl.pallas_call(
        _norm_mod_router_body, grid=(b, t // tm), in_specs=in_specs,
        out_specs=[h_spec, pl.BlockSpec((1, tm, N_EXPERTS), lambda b_, i: (b_, i, 0))],
        out_shape=[h_shape, jax.ShapeDtypeStruct((b, t, N_EXPERTS), F32)],
        compiler_params=_cparams(("arbitrary", "arbitrary"), 48), name="norm_mod_router",
    )(x3, g.reshape(1, d), mod3, mod3, router_w)


def _mm_body(x_ref, w_ref, o_ref):
    o_ref[...] = _dot(x_ref[...], w_ref[...]).astype(o_ref.dtype)


def _matmul(x, w, col0, n_cols, tm, tn, out_dtype=F32):
    m, k = x.shape
    assert col0 % tn == 0 and n_cols % tn == 0 and m % tm == 0
    off = col0 // tn
    return pl.pallas_call(
        _mm_body,
        grid=(n_cols // tn, m // tm),
        in_specs=[pl.BlockSpec((tm, k), lambda j, i: (i, 0)),
                  pl.BlockSpec((k, tn), lambda j, i: (0, j + off))],
        out_specs=pl.BlockSpec((tm, tn), lambda j, i: (i, j)),
        out_shape=jax.ShapeDtypeStruct((m, n_cols), out_dtype),
        compiler_params=_cparams(("arbitrary", "arbitrary"), 48),
        name="matmul",
    )(x, w)


def _mm_res_body(x_ref, w_ref, res_ref, g_ref, o_ref):
    o_ref[0] = res_ref[0] + g_ref[0] * _dot(x_ref[0], w_ref[...])


def _matmul_gated_residual(x3, w, res3, mod3, gate_chunk, tm, tn=512):
    b, t, k = x3.shape
    n = w.shape[1]
    nt = n // tn
    if mod3.shape[1] == 1:
        g_spec = pl.BlockSpec((1, 1, tn), lambda j, b_, i: (b_, 0, gate_chunk * nt + j))
    else:
        g_spec = pl.BlockSpec((1, tm, tn), lambda j, b_, i: (b_, i, gate_chunk * nt + j))
    return pl.pallas_call(
        _mm_res_body,
        grid=(nt, b, t // tm),
        in_specs=[pl.BlockSpec((1, tm, k), lambda j, b_, i: (b_, i, 0)),
                  pl.BlockSpec((k, tn), lambda j, b_, i: (0, j)),
                  pl.BlockSpec((1, tm, tn), lambda j, b_, i: (b_, i, j)),
                  g_spec],
        out_specs=pl.BlockSpec((1, tm, tn), lambda j, b_, i: (b_, i, j)),
        out_shape=jax.ShapeDtypeStruct((b, t, n), F32),
        compiler_params=_cparams(("arbitrary", "arbitrary", "arbitrary"), 48),
        name="matmul_gated_residual",
    )(x3, w, res3, mod3)


def _rel_bucket(dist):
    n_exact = NUM_BUCKETS // 2
    df = jnp.maximum(dist, 1).astype(F32)
    large = n_exact + (jnp.log(df / n_exact) / math.log(MAX_DISTANCE / n_exact)
                       * (NUM_BUCKETS - n_exact)).astype(jnp.int32)
    return jnp.where(dist < n_exact, dist, jnp.minimum(large, NUM_BUCKETS - 1))


def _distance_bias(rel_bias, seq):
    dist = np.arange(seq)
    mult = np.zeros(seq, np.float64)
    for window, dil in DILATIONS:
        mult += (dist % dil == 0) & (dist <= window)
    logm = jnp.asarray(np.log(np.maximum(mult, 1.0)), F32)
    bias = rel_bias.astype(F32)[_rel_bucket(jnp.asarray(dist, jnp.int32))] + logm[:, None]
    bias = jnp.where(jnp.asarray(mult > 0)[:, None], bias, NEG)
    return bias.T


def _toeplitz_bias(rel_bias, seq, tile):
    nd = seq // tile
    cb = _distance_bias(rel_bias, seq)
    cb_ext = jnp.concatenate([jnp.full((H_ATTN, tile), NEG, F32), cb], axis=1)
    a = np.arange(tile)
    idx = (np.arange(nd)[:, None, None] * tile + a[None, :, None] - a[None, None, :] + tile).astype(np.int32)
    return jnp.take(cb_ext, jnp.asarray(idx.reshape(-1)), axis=1).reshape(H_ATTN, nd, tile, tile)


def _attn_body(q_ref, k_ref, v_ref, b_ref, o_ref):
    qi = pl.program_id(2)
    q = (q_ref[0, 0].astype(F32) * ATTN_SCALE).astype(BF16)

    def body(j, carry):
        m, l, acc = carry
        rows = pl.ds(pl.multiple_of(j * ATTN_TILE, ATTN_TILE), ATTN_TILE)
        s = lax.dot_general(q, k_ref[0, 0, rows, :], (((1,), (1,)), ((), ())),
                            preferred_element_type=F32) + b_ref[0, qi - j]
        m_new = jnp.maximum(m, jnp.max(s, axis=-1, keepdims=True))
        alpha = jnp.exp(m - m_new)
        p = jnp.exp(s - m_new)
        l = alpha * l + jnp.sum(p, axis=-1, keepdims=True)
        acc = alpha * acc + _dot(p.astype(BF16), v_ref[0, 0, rows, :])
        return m_new, l, acc

    init = (jnp.full((ATTN_TILE, 1), NEG, F32), jnp.zeros((ATTN_TILE, 1), F32),
            jnp.zeros((ATTN_TILE, HEAD_DIM), F32))
    _, l, acc = lax.fori_loop(0, qi + 1, body, init)
    o_ref[0, 0] = (acc / l).astype(o_ref.dtype)


def _prompt_attention(q, k, v, rel_bias):
    b, h, s, e = q.shape
    tb = _toeplitz_bias(rel_bias, s, ATTN_TILE)
    return pl.pallas_call(
        _attn_body,
        grid=(b, h, s // ATTN_TILE),
        in_specs=[pl.BlockSpec((1, 1, ATTN_TILE, e), lambda b_, h_, i: (b_, h_, i, 0)),
                  pl.BlockSpec((1, 1, s, e), lambda b_, h_, i: (b_, h_, 0, 0)),
                  pl.BlockSpec((1, 1, s, e), lambda b_, h_, i: (b_, h_, 0, 0)),
                  pl.BlockSpec((1, s // ATTN_TILE, ATTN_TILE, ATTN_TILE), lambda b_, h_, i: (h_, 0, 0, 0))],
        out_specs=pl.BlockSpec((1, 1, ATTN_TILE, e), lambda b_, h_, i: (b_, h_, i, 0)),
        out_shape=jax.ShapeDtypeStruct((b, h, s, e), BF16),
        compiler_params=_cparams(("arbitrary", "arbitrary", "arbitrary"), 32),
        name="prompt_attention",
    )(q, k, v, tb)


def _head_segments(width, n_heads):
    seg = np.zeros((width, LANES), np.float32)
    seg[np.arange(width), np.arange(width) // HEAD_DIM] = 1.0
    assert n_heads <= LANES
    return jnp.asarray(seg), jnp.asarray(seg.T.copy())


def _sample_attn_body(q_ref, kn_ref, vn_ref, k1_ref, k4_ref, k16_ref, v1_ref, v4_ref, v16_ref,
                      bias_ref, b0_ref, seg_ref, segt_ref, o_ref):
    q = q_ref[0] * ATTN_SCALE
    seg = seg_ref[...]
    segt = segt_ref[...]
    s0 = _dot(kn_ref[0] * q, seg, HIGHEST) + b0_ref[...]
    scores = [_dot(kr[0] * q, seg, HIGHEST) + bias_ref[i] for i, kr in enumerate((k1_ref, k4_ref, k16_ref))]
    m = s0
    for s in scores:
        m = jnp.maximum(m, jnp.max(s, axis=0, keepdims=True))
    p0 = float(len(DILATIONS)) * jnp.exp(s0 - m)
    den = p0
    num = _dot(p0, segt, HIGHEST) * vn_ref[0]
    for s, vr in zip(scores, (v1_ref, v4_ref, v16_ref)):
        p = jnp.exp(s - m)
        den = den + jnp.sum(p, axis=0, keepdims=True)
        num = num + jnp.sum(_dot(p, segt, HIGHEST) * vr[0], axis=0, keepdims=True)
    o_ref[0] = (num / _dot(den, segt, HIGHEST)).astype(o_ref.dtype)


def _sample_attention(q, k_new, v_new, k_buf, v_buf, rel_bias):
    bd, _, w = q.shape
    wbuf = k_buf.shape[1]
    n = 128
    views, specs = [], []
    for buf in (k_buf, v_buf):
        for window, dil in DILATIONS:
            assert window // dil == n and wbuf % (n * dil) == 0 and window <= wbuf
            views.append(buf.reshape(bd, wbuf // dil, dil * w))
            last = wbuf // dil // n - 1
            specs.append(pl.BlockSpec((1, n, w), lambda b_, last=last: (b_, last, 0)))
    biases = []
    for window, dil in DILATIONS:
        dist = jnp.asarray(dil * (n - np.arange(n)), jnp.int32)
        biases.append(rel_bias.astype(F32)[_rel_bucket(dist)])
    bias = jnp.pad(jnp.stack(biases), ((0, 0), (0, 0), (0, LANES - H_ATTN)))
    b0 = jnp.pad(rel_bias.astype(F32)[_rel_bucket(jnp.zeros((1,), jnp.int32))], ((0, 0), (0, LANES - H_ATTN)))
    seg, segt = _head_segments(w, H_ATTN)
    row = pl.BlockSpec((1, 1, w), lambda b_: (b_, 0, 0))
    return pl.pallas_call(
        _sample_attn_body,
        grid=(bd,),
        in_specs=[row, row, row] + specs + [
            pl.BlockSpec((len(DILATIONS), n, LANES), lambda b_: (0, 0, 0)),
            pl.BlockSpec((1, LANES), lambda b_: (0, 0)),
            pl.BlockSpec((w, LANES), lambda b_: (0, 0)),
            pl.BlockSpec((LANES, w), lambda b_: (0, 0))],
        out_specs=row,
        out_shape=jax.ShapeDtypeStruct((bd, 1, w), BF16),
        compiler_params=_cparams(("arbitrary",), 40),
        name="sample_attention",
    )(q, k_new, v_new, *views, bias, b0, seg, segt)


def _softplus(z):
    return jnp.maximum(z, 0.0) + jnp.log1p(jnp.exp(-jnp.abs(z)))


def _rwkv_prep_body(r_ref, k_ref, v_ref, t_ref, rp_ref, kp_ref, vp_ref, tp_ref,
                    s0r_ref, s0k_ref, s0v_ref, s0t_ref, mur_ref, muk_ref, muv_ref, mut_ref,
                    w0_ref, a0_ref, kk_ref, ka_ref, w2_ref, a2_ref, g2_ref, seg_ref, segt_ref,
                    r_o, w_o, k_o, v_o, nkk_o, kka_o, g_o, *, single_step):
    first = pl.program_id(1) == 0

    def mix(cur_ref, prev_ref, s0_ref, mu_ref):
        cur = cur_ref[0]
        if single_step:
            shifted = s0_ref[0]
        else:
            prev_row = jnp.where(first, s0_ref[0], prev_ref[0, 7:8, :])
            row = lax.broadcasted_iota(jnp.int32, cur.shape, 0)
            shifted = jnp.where(row == 0, prev_row, pltpu.roll(cur, 1, 0))
        return cur + mu_ref[...] * (shifted - cur)

    r = mix(r_ref, rp_ref, s0r_ref, mur_ref)
    k = mix(k_ref, kp_ref, s0k_ref, muk_ref)
    v = mix(v_ref, vp_ref, s0v_ref, muv_ref)
    t = mix(t_ref, tp_ref, s0t_ref, mut_ref)
    lw = t[:, :DECAY_LORA]
    la = t[:, DECAY_LORA:DECAY_LORA + AAA_LORA]
    lg = t[:, DECAY_LORA + AAA_LORA:]
    w_log = -_softplus(-(w0_ref[...] + _dot(jnp.tanh(lw), w2_ref[...], HIGHEST))) - 0.5
    a = jax.nn.sigmoid(a0_ref[...] + _dot(la, a2_ref[...], HIGHEST))
    kk = k * kk_ref[...]
    ssq = _dot(_dot(kk * kk, seg_ref[...], HIGHEST), segt_ref[...], HIGHEST)
    kkn = kk / jnp.maximum(jnp.sqrt(ssq), 1e-12)
    r_o[0] = r
    w_o[0] = jnp.exp(-jnp.exp(w_log))
    k_o[0] = k * (1.0 + (a - 1.0) * ka_ref[...])
    v_o[0] = v
    nkk_o[0] = -kkn
    kka_o[0] = kkn * a
    g_o[0] = _dot(jax.nn.sigmoid(lg), g2_ref[...])


def _rwkv_prep(main3, tail3, shift0, tmix_mu, w0, w2, a0, a2, g2, k_k, k_a, tm, single_step):
    b, t, _ = main3.shape
    rw = RWKV_WIDTH
    seg, segt = _head_segments(rw, H_RWKV)
    s0 = shift0.reshape(b, -1, RWKV_COLS)
    s0_main, s0_tail = s0[..., :RWKV_MAIN], s0[..., RWKV_MAIN:]
    mu = tmix_mu.reshape(1, RWKV_COLS)
    mu_main, mu_tail = mu[:, :RWKV_MAIN], mu[:, RWKV_MAIN:]
    prev = 8 if t >= 8 else t

    def cur(c, width=rw):
        return pl.BlockSpec((1, tm, width), lambda b_, i: (b_, i, c))

    def prv(c, width=rw):
        return pl.BlockSpec((1, prev, width), lambda b_, i: (b_, jnp.maximum(i * (tm // prev) - 1, 0), c))

    def s0s(c, width=rw):
        if single_step:
            return cur(c, width)
        return pl.BlockSpec((1, 1, width), lambda b_, i: (b_, 0, c))

    def par(c, rows=1, width=rw):
        return pl.BlockSpec((rows, width), lambda b_, i: (0, c))

    vec = lambda x: x.reshape(1, rw)
    out_spec = pl.BlockSpec((1, tm, rw), lambda b_, i: (b_, i, 0))
    out_shape = jax.ShapeDtypeStruct((b, t, rw), F32)
    return pl.pallas_call(
        functools.partial(_rwkv_prep_body, single_step=single_step),
        grid=(b, t // tm),
        in_specs=[cur(0), cur(1), cur(2), cur(0, LORA_WIDTH),
                  prv(0), prv(1), prv(2), prv(0, LORA_WIDTH),
                  s0s(0), s0s(1), s0s(2), s0s(0, LORA_WIDTH),
                  par(0), par(1), par(2), par(0, 1, LORA_WIDTH),
                  par(0), par(0), par(0), par(0),
                  par(0, DECAY_LORA), par(0, AAA_LORA), par(0, GATE_LORA),
                  pl.BlockSpec((rw, LANES), lambda b_, i: (0, 0)),
                  pl.BlockSpec((LANES, rw), lambda b_, i: (0, 0))],
        out_specs=[out_spec] * 7,
        out_shape=[out_shape] * 7,
        compiler_params=_cparams(("arbitrary", "arbitrary"), 56),
        name="rwkv_prep",
    )(main3, main3, main3, tail3, main3, main3, main3, tail3,
      s0_main, s0_main, s0_main, s0_tail, mu_main, mu_main, mu_main, mu_tail,
      vec(w0), vec(a0), vec(k_k), vec(k_a), w2, a2, g2, seg, segt)


def _rwkv_scan_body(nkk_ref, w_ref, kka_ref, k_ref, v_ref, r_ref, s0_ref, y_ref, sf_ref, s_ref, *, tt):
    tb = pl.program_id(1)

    @pl.when(tb == 0)
    def _():
        s_ref[...] = s0_ref[...]

    def step(t, _):
        nkk = nkk_ref[t]
        w = w_ref[t]
        kka = kka_ref[t]
        kv = k_ref[t]
        r = r_ref[t]

        def rows8(vc, _):
            ys = []
            for u in range(8):
                vi = vc * 8 + u
                s_old = s_ref[vi]
                sa = jnp.sum(s_old * nkk, axis=0, keepdims=True)
                s_new = s_old * w + sa * kka + v_ref[t, pl.ds(vi, 1), :] * kv
                s_ref[vi] = s_new
                ys.append(jnp.sum(s_new * r, axis=0, keepdims=True))
            y_ref[t, pl.ds(pl.multiple_of(vc * 8, 8), 8), :] = jnp.concatenate(ys, axis=0)
            return 0

        lax.fori_loop(0, HEAD_DIM // 8, rows8, 0)
        return 0

    lax.fori_loop(0, tt, step, 0)

    @pl.when(tb == pl.num_programs(1) - 1)
    def _():
        sf_ref[...] = s_ref[...]


def _rwkv_scan(nkk, w, kka, k, v, r, s0, tt):
    t, e, lanes = r.shape
    seq = pl.BlockSpec((tt, e, LANES), lambda g, i: (i, 0, g))
    st = pl.BlockSpec((e, e, LANES), lambda g, i: (0, 0, g))
    return pl.pallas_call(
        functools.partial(_rwkv_scan_body, tt=tt),
        grid=(lanes // LANES, t // tt),
        in_specs=[seq] * 6 + [st],
        out_specs=[seq, st],
        out_shape=[jax.ShapeDtypeStruct((t, e, lanes), F32), jax.ShapeDtypeStruct((e, e, lanes), F32)],
        scratch_shapes=[pltpu.VMEM((e, e, LANES), F32)],
        compiler_params=_cparams(("arbitrary", "arbitrary"), 40),
        name="rwkv_scan",
    )(nkk, w, kka, k, v, r, s0)


def _rwkv_post_body(y_ref, r_ref, k_ref, v_ref, g_ref, lnw_ref, lnb_ref, rk_ref, seg_ref, segt_ref, o_ref):
    seg = seg_ref[...]
    segt = segt_ref[...]
    head_sum = lambda x: _dot(_dot(x, seg, HIGHEST), segt, HIGHEST)
    y = y_ref[0]
    yc = y - head_sum(y) * (1.0 / HEAD_DIM)
    var = head_sum(yc * yc) * (1.0 / HEAD_DIM)
    yn = yc * lax.rsqrt(var + GN_EPS) * lnw_ref[...] + lnb_ref[...]
    bonus = head_sum(r_ref[0] * k_ref[0] * rk_ref[...]) * v_ref[0]
    o_ref[0] = ((yn + bonus) * g_ref[0]).astype(o_ref.dtype)


def _rwkv_post(y, r, k, v, g, ln_w, ln_b, r_k, tm):
    b, t, rw = y.shape
    seg, segt = _head_segments(rw, H_RWKV)
    blk = pl.BlockSpec((1, tm, rw), lambda b_, i: (b_, i, 0))
    par = pl.BlockSpec((1, rw), lambda b_, i: (0, 0))
    return pl.pallas_call(
        _rwkv_post_body,
        grid=(b, t // tm),
        in_specs=[blk] * 5 + [par] * 3 + [pl.BlockSpec((rw, LANES), lambda b_, i: (0, 0)),
                                          pl.BlockSpec((LANES, rw), lambda b_, i: (0, 0))],
        out_specs=blk,
        out_shape=jax.ShapeDtypeStruct((b, t, rw), BF16),
        compiler_params=_cparams(("arbitrary", "arbitrary"), 40),
        name="rwkv_post",
    )(y, r, k, v, g, ln_w.reshape(1, rw), ln_b.reshape(1, rw), r_k.reshape(1, rw), seg, segt)


def _to_scan_layout(x, lanes):
    b, t, _ = x.shape
    y = x.reshape(b, t, H_RWKV, HEAD_DIM).transpose(1, 3, 0, 2).reshape(t, HEAD_DIM, b * H_RWKV)
    return jnp.pad(y, ((0, 0), (0, 0), (0, lanes - b * H_RWKV)))


def _rwkv_time_mix(main3, tail3, shift0, state0, tmix_mu, w0, w2, a0, a2, g2, k_k, k_a, r_k, ln_w, ln_b,
                   tm, tt, single_step):
    b, t, _ = main3.shape
    r, w, k, v, nkk, kka, g = _rwkv_prep(main3, tail3, shift0, tmix_mu, w0, w2, a0, a2, g2, k_k, k_a, tm, single_step)
    nseq, steps = (t, 1) if single_step else (b, t)
    n = nseq * H_RWKV
    lanes = -(-n // LANES) * LANES
    s0 = state0.astype(F32).reshape(n, HEAD_DIM, HEAD_DIM).transpose(1, 2, 0)
    s0 = jnp.pad(s0, ((0, 0), (0, 0), (0, lanes - n)))
    seqs = (_to_scan_layout(x.reshape(nseq, steps, RWKV_WIDTH), lanes) for x in (nkk, w, kka, k, v, r))
    ys, sf = _rwkv_scan(*seqs, s0, tt)
    y = ys[:, :, :n].reshape(steps, HEAD_DIM, nseq, H_RWKV).transpose(2, 0, 3, 1).reshape(b, t, RWKV_WIDTH)
    state = sf[:, :, :n].transpose(2, 0, 1).reshape(nseq, H_RWKV, HEAD_DIM, HEAD_DIM)
    return _rwkv_post(y, r, k, v, g, ln_w, ln_b, r_k, tm), state


def _moe_up_body(be_ref, x_ref, wg_ref, wu_ref, h_ref):
    x = x_ref[...]
    g = _dot(x, wg_ref[0])
    h_ref[...] = (g * jax.nn.sigmoid(g) * _dot(x, wu_ref[0])).astype(h_ref.dtype)


def _moe_down_body(be_ref, h_ref, wd_ref, rw_ref, y_ref):
    y_ref[...] = (_dot(h_ref[...], wd_ref[0]) * rw_ref[...]).astype(y_ref.dtype)


def _grouped_mlp(xs, blk_expert, row_w, w_gate, w_up, w_down, tf=512, tn=2048):
    rows, d = xs.shape
    nblk = rows // MOE_ROWS
    ff = w_gate.shape[2]
    h = pl.pallas_call(
        _moe_up_body,
        grid_spec=pltpu.PrefetchScalarGridSpec(
            num_scalar_prefetch=1,
            grid=(ff // tf, nblk),
            in_specs=[pl.BlockSpec((MOE_ROWS, d), lambda f, m, be: (m, 0)),
                      pl.BlockSpec((1, d, tf), lambda f, m, be: (be[m], 0, f)),
                      pl.BlockSpec((1, d, tf), lambda f, m, be: (be[m], 0, f))],
            out_specs=pl.BlockSpec((MOE_ROWS, tf), lambda f, m, be: (m, f))),
        out_shape=jax.ShapeDtypeStruct((rows, ff), BF16),
        compiler_params=_cparams(("arbitrary", "arbitrary"), 48),
        name="moe_up",
    )(blk_expert, xs, w_gate, w_up)
    return pl.pallas_call(
        _moe_down_body,
        grid_spec=pltpu.PrefetchScalarGridSpec(
            num_scalar_prefetch=1,
            grid=(d // tn, nblk),
            in_specs=[pl.BlockSpec((MOE_ROWS, ff), lambda n, m, be: (m, 0)),
                      pl.BlockSpec((1, ff, tn), lambda n, m, be: (be[m], 0, n)),
                      pl.BlockSpec((MOE_ROWS, 1), lambda n, m, be: (m, 0))],
            out_specs=pl.BlockSpec((MOE_ROWS, tn), lambda n, m, be: (m, n))),
        out_shape=jax.ShapeDtypeStruct((rows, d), BF16),
        compiler_params=_cparams(("arbitrary", "arbitrary"), 48),
        name="moe_down",
    )(blk_expert, h, w_down, row_w)


def _route(scores, router_b):
    n = scores.shape[0]
    choice = scores + router_b
    grp = jnp.sum(lax.top_k(choice.reshape(n, N_GROUPS, N_EXPERTS // N_GROUPS), 2)[0], axis=-1)
    _, gidx = lax.top_k(grp, TOPK_GROUPS)
    gmask = jnp.any(gidx[..., None] == jnp.arange(N_GROUPS), axis=1)
    choice = jnp.where(jnp.repeat(gmask, N_EXPERTS // N_GROUPS, axis=1), choice, -jnp.inf)
    _, eidx = lax.top_k(choice, TOP_K)
    sel = jnp.take_along_axis(scores, eidx, axis=1)
    gate = sel / jnp.sum(sel, axis=-1, keepdims=True) * ROUTED_SCALE
    return eidx, gate


def _dispatch_plan(eidx, gate, n_blocks, zero_row):
    n = eidx.shape[0]
    onehot = jnp.sum((eidx[:, :, None] == jnp.arange(N_EXPERTS, dtype=jnp.int32)).astype(jnp.int32), axis=1)
    before = jnp.cumsum(onehot, axis=0) - onehot
    rank = jnp.take_along_axis(before, eidx, axis=1)
    counts = jnp.sum(onehot, axis=0)
    padded = (counts + MOE_ROWS - 1) // MOE_ROWS * MOE_ROWS
    pad_end = jnp.cumsum(padded)
    dest = (pad_end - padded)[eidx] + rank
    rows = n_blocks * MOE_ROWS
    tok = jnp.broadcast_to(jnp.arange(n, dtype=jnp.int32)[:, None], dest.shape)
    row_tok = jnp.full((rows,), zero_row, jnp.int32).at[dest.reshape(-1)].set(tok.reshape(-1))
    row_w = jnp.zeros((rows,), F32).at[dest.reshape(-1)].set(gate.reshape(-1))
    blk_start = jnp.arange(n_blocks, dtype=jnp.int32) * MOE_ROWS
    blk_expert = jnp.minimum(jnp.searchsorted(pad_end, blk_start, side="right"), N_EXPERTS - 1).astype(jnp.int32)
    return dest, row_tok, row_w, blk_expert


def _final_body(x_ref, routed_ref, shared_ref, gt_ref, fn_ref, o_ref):
    x = x_ref[0] + gt_ref[0] * (routed_ref[0] + shared_ref[0].astype(F32))
    o_ref[0] = x * lax.rsqrt(jnp.mean(x * x, axis=-1, keepdims=True) + RMS_EPS) * fn_ref[...]


def _final(x3, routed3, shared3, mod3, gate_chunk, final_norm, tm):
    b, t, d = x3.shape
    blk = pl.BlockSpec((1, tm, d), lambda b_, i: (b_, i, 0))
    return pl.pallas_call(
        _final_body,
        grid=(b, t // tm),
        in_specs=[blk, blk, blk, _mod_spec(mod3, gate_chunk, tm), pl.BlockSpec((1, d), lambda b_, i: (0, 0))],
        out_specs=blk,
        out_shape=jax.ShapeDtypeStruct((b, t, d), F32),
        compiler_params=_cparams(("arbitrary", "arbitrary"), 48),
        name="final",
    )(x3, routed3, shared3, mod3, final_norm.reshape(1, d))


def _project(x3, mod3, norm1, w_in, tm_norm, tm):
    b, t, d = x3.shape
    h = _norm_mod(x3, norm1, mod3, 1, 0, tm_norm).reshape(b * t, d)
    tn = 512
    q, k, v = (_matmul(h, w_in, i * ATTN_WIDTH, ATTN_WIDTH, tm, tn) for i in range(3))
    main = _matmul(h, w_in, ATTN_COLS, RWKV_MAIN, tm, tn)
    tail = _matmul(h, w_in[:, ATTN_COLS + RWKV_MAIN:], 0, LORA_WIDTH, tm, LORA_WIDTH)
    return q, k, v, main, tail


def kernel(x_prompt, x_sample, cache_attn_k, cache_attn_v, state_rwkv, state_shift, c_prompt, c_sample,
           rel_bias, norm1, norm2, final_norm, w_ada, b_ada, w_in, w_out, tmix_mu, w0, w2, a0, a2, g2,
           k_k, k_a, r_k, ln_w, ln_b, router_w, router_b, ws_gate, ws_up, ws_down, w_gate, w_up, w_down):
    bp, sp, d = x_prompt.shape
    bs = x_sample.shape[0]
    assert w_in.shape[0] == 1
    l = 0

    c_all = jnp.concatenate([c_prompt, c_sample, jnp.zeros((4, d), F32)], axis=0)
    mod = _ada_mod(c_all, w_ada[l], b_ada[l])
    mod_p = mod[:bp].reshape(bp, 1, 6 * d)
    mod_s = mod[bp:bp + bs].reshape(1, bs, 6 * d)
    xs3 = x_sample.reshape(1, bs, d)
    tm_p, tm_s, tm_norm = 512, bs, 256

    rwkv_par = (tmix_mu[l], w0[l], w2[l], a0[l], a2[l], g2[l], k_k[l], k_a[l], r_k[l], ln_w[l], ln_b[l])

    qp, kp, vp, main_p, tail_p = _project(x_prompt, mod_p, norm1[l], w_in[l], tm_norm, tm_p)
    heads = lambda a: a.reshape(bp, sp, H_ATTN, HEAD_DIM).transpose(0, 2, 1, 3).astype(BF16)
    o_attn_p = _prompt_attention(heads(qp), heads(kp), heads(vp), rel_bias)
    o_attn_p = o_attn_p.transpose(0, 2, 1, 3).reshape(bp, sp, ATTN_WIDTH)
    main_p3 = main_p.reshape(bp, sp, RWKV_MAIN)
    tail_p3 = tail_p.reshape(bp, sp, LORA_WIDTH)
    o_rwkv_p, state_p = _rwkv_time_mix(main_p3, tail_p3, jnp.zeros((bp, RWKV_COLS), F32),
                                       jnp.zeros((bp, H_RWKV, HEAD_DIM, HEAD_DIM), F32), *rwkv_par,
                                       tm=128, tt=32, single_step=False)
    shift_p = jnp.concatenate([main_p3[:, -1], tail_p3[:, -1]], axis=-1)
    x1_p = _matmul_gated_residual(jnp.concatenate([o_attn_p, o_rwkv_p], axis=-1), w_out[l], x_prompt, mod_p, 2, tm_p)
    h2_p, sc_p = _norm_mod(x1_p, norm2[l], mod_p, 4, 3, tm_norm, router_w[l])

    qs, ks, vs, main_s, tail_s = _project(xs3, mod_s, norm1[l], w_in[l], tm_s, tm_s)
    row3 = lambda a: a.reshape(bs, 1, a.shape[-1])
    o_attn_s = _sample_attention(row3(qs), row3(ks), row3(vs),
                                 cache_attn_k[l].reshape(bs, -1, ATTN_WIDTH),
                                 cache_attn_v[l].reshape(bs, -1, ATTN_WIDTH), rel_bias)
    o_rwkv_s, state_s = _rwkv_time_mix(main_s.reshape(1, bs, RWKV_MAIN), tail_s.reshape(1, bs, LORA_WIDTH),
                                       state_shift[l].reshape(1, bs, RWKV_COLS), state_rwkv[l], *rwkv_par,
                                       tm=bs, tt=1, single_step=True)
    shift_s = jnp.concatenate([main_s, tail_s], axis=-1)
    o_cat_s = jnp.concatenate([o_attn_s.reshape(1, bs, ATTN_WIDTH), o_rwkv_s], axis=-1)
    x1_s = _matmul_gated_residual(o_cat_s, w_out[l], xs3, mod_s, 2, tm_s)
    h2_s, sc_s = _norm_mod(x1_s, norm2[l], mod_s, 4, 3, tm_s, router_w[l])

    n_p, n_s = bp * sp, bs
    n_all = n_p + n_s
    n_pad = -(-(n_all + 1) // MOE_ROWS) * MOE_ROWS
    h2_all = jnp.concatenate([h2_p.reshape(n_p, d), h2_s.reshape(n_s, d),
                              jnp.zeros((n_pad - n_all, d), BF16)], axis=0)
    scores = jnp.concatenate([sc_p.reshape(n_p, N_EXPERTS), sc_s.reshape(n_s, N_EXPERTS)], axis=0)
    eidx, gate = _route(scores, router_b[l])
    n_blocks = -(-(n_all * TOP_K) // MOE_ROWS) + N_EXPERTS
    dest, row_tok, row_w, blk_expert = _dispatch_plan(eidx, gate, n_blocks, n_all)
    yb = _grouped_mlp(h2_all[row_tok], blk_expert, row_w.reshape(-1, 1), w_gate[l], w_up[l], w_down[l])
    routed = jnp.sum(yb[dest].astype(F32), axis=1)
    shared = _grouped_mlp(h2_all, jnp.zeros((n_pad // MOE_ROWS,), jnp.int32), jnp.ones((n_pad, 1), F32),
                          ws_gate, ws_up, ws_down)

    y_p = _final(x1_p, routed[:n_p].reshape(bp, sp, d), shared[:n_p].reshape(bp, sp, d), mod_p, 5, final_norm, tm_norm)
    y_s = _final(x1_s, routed[n_p:].reshape(1, bs, d), shared[n_p:n_all].reshape(1, bs, d), mod_s, 5, final_norm, tm_s)

    return (y_p, y_s.reshape(bs, 1, d),
            kp.reshape(1, bp, sp, H_ATTN, HEAD_DIM), vp.reshape(1, bp, sp, H_ATTN, HEAD_DIM),
            state_p[None], shift_p[None],
            ks.reshape(1, bs, 1, H_ATTN, HEAD_DIM), vs.reshape(1, bs, 1, H_ATTN, HEAD_DIM),
            state_s[None], shift_s[None])
```

```python
import functools
import math

import numpy as np
import jax
import jax.numpy as jnp
from jax import lax
from jax.experimental import pallas as pl
from jax.experimental.pallas import tpu as pltpu

F32 = jnp.float32
BF16 = jnp.bfloat16
HIGHEST = lax.Precision.HIGHEST

D_MODEL = 4096
HEAD_DIM = 64
H_ATTN = 24
ATTN_WIDTH = H_ATTN * HEAD_DIM
H_RWKV = 40
RWKV_WIDTH = H_RWKV * HEAD_DIM
DILATIONS = ((128, 1), (512, 4), (2048, 16))
NUM_BUCKETS = 32
MAX_DISTANCE = 2048
DECAY_LORA = 128
AAA_LORA = 128
GATE_LORA = 480
LORA_WIDTH = DECAY_LORA + AAA_LORA + GATE_LORA
ATTN_COLS = 3 * ATTN_WIDTH
RWKV_MAIN = 3 * RWKV_WIDTH
RWKV_COLS = RWKV_MAIN + LORA_WIDTH
N_EXPERTS = 256
TOP_K = 8
N_GROUPS = 8
TOPK_GROUPS = 4
EXPERT_FF = 1024
ROUTED_SCALE = 2.5
RMS_EPS = 1e-6
GN_EPS = 64e-5
ATTN_SCALE = HEAD_DIM ** -0.5
NEG = -1e30

LANES = 128
SLAB_ROWS = D_MODEL // 2 // LANES
MOE_ROWS = 128
ATTN_TILE = 256
MIB = 1 << 20


def _cparams(dims, vmem_mib=None):
    kw = dict(dimension_semantics=dims)
    if vmem_mib is not None:
        kw["vmem_limit_bytes"] = vmem_mib * MIB
    return pltpu.CompilerParams(**kw)


def _dot(a, b, precision=None):
    return lax.dot_general(a, b, (((a.ndim - 1,), (0,)), ((), ())),
                           precision=precision, preferred_element_type=F32)


def _ada_body(c_ref, w_ref, b_ref, o_ref):
    c = c_ref[...]
    o_ref[...] = _dot(c * jax.nn.sigmoid(c), w_ref[...]) + b_ref[...]


def _ada_mod(c_all, w_ada, b_ada, tn=512):
    m, k = c_all.shape
    n = w_ada.shape[1]
    return pl.pallas_call(
        _ada_body,
        grid=(n // tn,),
        in_specs=[pl.BlockSpec((m, k), lambda j: (0, 0)),
                  pl.BlockSpec((k, tn), lambda j: (0, j)),
                  pl.BlockSpec((1, tn), lambda j: (0, j))],
        out_specs=pl.BlockSpec((m, tn), lambda j: (0, j)),
        out_shape=jax.ShapeDtypeStruct((m, n), F32),
        compiler_params=_cparams(("arbitrary",), 40),
        name="ada_mod",
    )(c_all, w_ada, b_ada.reshape(1, n))


def _norm_mod_body(x_ref, g_ref, sc_ref, sh_ref, o_ref):
    x = x_ref[0]
    y = x * lax.rsqrt(jnp.mean(x * x, axis=-1, keepdims=True) + RMS_EPS) * g_ref[...]
    o_ref[0] = (y * (1.0 + sc_ref[0]) + sh_ref[0]).astype(o_ref.dtype)


def _pack_bf16_pairs(h):
    bits = lax.bitcast_convert_type(h.astype(BF16).astype(F32), jnp.uint32)
    half = h.shape[1] // 2
    return (bits[:, half:] & jnp.uint32(0xFFFF0000)) | (bits[:, :half] >> 16)


def _unpack_bf16_pairs(w):
    return (lax.bitcast_convert_type(w << 16, F32),
            lax.bitcast_convert_type(w & jnp.uint32(0xFFFF0000), F32))


def _store_slabs(ref, words):
    for j in range(SLAB_ROWS):
        ref[:, j, :] = words[:, j * LANES:(j + 1) * LANES]


def _load_slabs(ref):
    return jnp.concatenate([ref[:, j, :] for j in range(SLAB_ROWS)], axis=1)


def _top_k_route(scores, router_b):
    m = scores.shape[0]
    gsz = N_EXPERTS // N_GROUPS
    lane = lax.broadcasted_iota(jnp.int32, scores.shape, 1).astype(F32)
    grp = jnp.floor(lane * (1.0 / gsz))
    choice = scores + router_b
    ninf = -jnp.inf
    row_max = lambda x: jnp.max(x, axis=-1, keepdims=True)
    row_min = lambda x: jnp.min(x, axis=-1, keepdims=True)
    gscore = jnp.zeros_like(choice)
    for g in range(N_GROUPS):
        ing = grp == float(g)
        cg = jnp.where(ing, choice, ninf)
        m1 = row_max(cg)
        i1 = row_min(jnp.where(cg == m1, lane, float(N_EXPERTS)))
        m2 = row_max(jnp.where(lane == i1, ninf, cg))
        gscore = jnp.where(ing, m1 + m2, gscore)
    cand = jnp.full_like(choice, ninf)
    for _ in range(TOPK_GROUPS):
        best = row_max(gscore)
        gsel = row_min(jnp.where(gscore == best, grp, float(N_GROUPS)))
        hit = grp == gsel
        cand = jnp.where(hit, choice, cand)
        gscore = jnp.where(hit, ninf, gscore)
    out_lane = lax.broadcasted_iota(jnp.int32, (m, LANES), 1)
    ids = jnp.zeros((m, LANES), F32)
    sel = jnp.zeros((m, LANES), F32)
    for k in range(TOP_K):
        best = row_max(cand)
        idx = row_min(jnp.where(cand == best, lane, float(N_EXPERTS)))
        hit = lane == idx
        ids = jnp.where(out_lane == k, idx, ids)
        sel = jnp.where(out_lane == k, jnp.sum(jnp.where(hit, scores, 0.0), axis=-1, keepdims=True), sel)
        cand = jnp.where(hit, ninf, cand)
    return ids, sel / jnp.sum(sel, axis=-1, keepdims=True) * ROUTED_SCALE


def _norm_mod_router_body(x_ref, g_ref, sc_ref, sh_ref, rw_ref, rb_ref, o_ref, e_ref, w_ref):
    x = x_ref[0]
    y = x * lax.rsqrt(jnp.mean(x * x, axis=-1, keepdims=True) + RMS_EPS) * g_ref[...]
    h = y * (1.0 + sc_ref[0]) + sh_ref[0]
    _store_slabs(o_ref, _pack_bf16_pairs(h))
    ids, gate = _top_k_route(jax.nn.sigmoid(_dot(h, rw_ref[...])), rb_ref[...])
    e_ref[0] = ids.astype(jnp.int32)
    w_ref[0] = gate


def _mod_spec(mod3, chunk, tm):
    if mod3.shape[1] == 1:
        return pl.BlockSpec((1, 1, D_MODEL), lambda b, i: (b, 0, chunk))
    return pl.BlockSpec((1, tm, D_MODEL), lambda b, i: (b, i, chunk))


def _norm_mod(x3, g, mod3, sc_chunk, sh_chunk, tm, router=None):
    b, t, d = x3.shape
    in_specs = [pl.BlockSpec((1, tm, d), lambda b_, i: (b_, i, 0)),
                pl.BlockSpec((1, d), lambda b_, i: (0, 0)),
                _mod_spec(mod3, sc_chunk, tm),
                _mod_spec(mod3, sh_chunk, tm)]
    h_spec = pl.BlockSpec((1, tm, d), lambda b_, i: (b_, i, 0))
    h_shape = jax.ShapeDtypeStruct((b, t, d), BF16)
    if router is None:
        return pl.pallas_call(
            _norm_mod_body, grid=(b, t // tm), in_specs=in_specs, out_specs=h_spec, out_shape=h_shape,
            compiler_params=_cparams(("arbitrary", "arbitrary"), 40), name="norm_mod",
        )(x3, g.reshape(1, d), mod3, mod3)
    router_w, router_b = router
    nt = t // tm
    in_specs += [pl.BlockSpec((d, N_EXPERTS), lambda b_, i: (0, 0)),
                 pl.BlockSpec((1, N_EXPERTS), lambda b_, i: (0, 0))]
    sel_spec = pl.BlockSpec((1, tm, LANES), lambda b_, i: (b_, i, 0))
    return pl.pallas_call(
        _norm_mod_router_body, grid=(b, nt), in_specs=in_specs,
        out_specs=[pl.BlockSpec((tm, SLAB_ROWS, LANES), lambda b_, i: (b_ * nt + i, 0, 0)), sel_spec, sel_spec],
        out_shape=[jax.ShapeDtypeStruct((b * t, SLAB_ROWS, LANES), jnp.uint32),
                   jax.ShapeDtypeStruct((b, t, LANES), jnp.int32),
                   jax.ShapeDtypeStruct((b, t, LANES), F32)],
        compiler_params=_cparams(("arbitrary", "arbitrary"), 48), name="norm_mod_router",
    )(x3, g.reshape(1, d), mod3, mod3, router_w, router_b.reshape(1, N_EXPERTS))


def _mm_body(x_ref, w_ref, o_ref):
    o_ref[...] = _dot(x_ref[...], w_ref[...]).astype(o_ref.dtype)


def _matmul(x, w, col0, n_cols, tm, tn, out_dtype=F32):
    m, k = x.shape
    assert col0 % tn == 0 and n_cols % tn == 0 and m % tm == 0
    off = col0 // tn
    return pl.pallas_call(
        _mm_body,
        grid=(n_cols // tn, m // tm),
        in_specs=[pl.BlockSpec((tm, k), lambda j, i: (i, 0)),
                  pl.BlockSpec((k, tn), lambda j, i: (0, j + off))],
        out_specs=pl.BlockSpec((tm, tn), lambda j, i: (i, j)),
        out_shape=jax.ShapeDtypeStruct((m, n_cols), out_dtype),
        compiler_params=_cparams(("arbitrary", "arbitrary"), 48),
        name="matmul",
    )(x, w)


def _mm_res_body(x_ref, w_ref, res_ref, g_ref, o_ref):
    o_ref[0] = res_ref[0] + g_ref[0] * _dot(x_ref[0], w_ref[...])


def _matmul_gated_residual(x3, w, res3, mod3, gate_chunk, tm, tn=512):
    b, t, k = x3.shape
    n = w.shape[1]
    nt = n // tn
    if mod3.shape[1] == 1:
        g_spec = pl.BlockSpec((1, 1, tn), lambda j, b_, i: (b_, 0, gate_chunk * nt + j))
    else:
        g_spec = pl.BlockSpec((1, tm, tn), lambda j, b_, i: (b_, i, gate_chunk * nt + j))
    return pl.pallas_call(
        _mm_res_body,
        grid=(nt, b, t // tm),
        in_specs=[pl.BlockSpec((1, tm, k), lambda j, b_, i: (b_, i, 0)),
                  pl.BlockSpec((k, tn), lambda j, b_, i: (0, j)),
                  pl.BlockSpec((1, tm, tn), lambda j, b_, i: (b_, i, j)),
                  g_spec],
        out_specs=pl.BlockSpec((1, tm, tn), lambda j, b_, i: (b_, i, j)),
        out_shape=jax.ShapeDtypeStruct((b, t, n), F32),
        compiler_params=_cparams(("arbitrary", "arbitrary", "arbitrary"), 48),
        name="matmul_gated_residual",
    )(x3, w, res3, mod3)


def _rel_bucket(dist):
    n_exact = NUM_BUCKETS // 2
    df = jnp.maximum(dist, 1).astype(F32)
    large = n_exact + (jnp.log(df / n_exact) / math.log(MAX_DISTANCE / n_exact)
                       * (NUM_BUCKETS - n_exact)).astype(jnp.int32)
    return jnp.where(dist < n_exact, dist, jnp.minimum(large, NUM_BUCKETS - 1))


def _distance_bias(rel_bias, seq):
    dist = np.arange(seq)
    mult = np.zeros(seq, np.float64)
    for window, dil in DILATIONS:
        mult += (dist % dil == 0) & (dist <= window)
    logm = jnp.asarray(np.log(np.maximum(mult, 1.0)), F32)
    bias = rel_bias.astype(F32)[_rel_bucket(jnp.asarray(dist, jnp.int32))] + logm[:, None]
    bias = jnp.where(jnp.asarray(mult > 0)[:, None], bias, NEG)
    return bias.T


def _toeplitz_bias(rel_bias, seq, tile):
    nd = seq // tile
    cb = _distance_bias(rel_bias, seq)
    cb_ext = jnp.concatenate([jnp.full((H_ATTN, tile), NEG, F32), cb], axis=1)
    a = np.arange(tile)
    idx = (np.arange(nd)[:, None, None] * tile + a[None, :, None] - a[None, None, :] + tile).astype(np.int32)
    return jnp.take(cb_ext, jnp.asarray(idx.reshape(-1)), axis=1).reshape(H_ATTN, nd, tile, tile)


def _attn_body(q_ref, k_ref, v_ref, b_ref, o_ref):
    qi = pl.program_id(2)
    q = (q_ref[0, 0].astype(F32) * ATTN_SCALE).astype(BF16)

    def body(j, carry):
        m, l, acc = carry
        rows = pl.ds(pl.multiple_of(j * ATTN_TILE, ATTN_TILE), ATTN_TILE)
        s = lax.dot_general(q, k_ref[0, 0, rows, :], (((1,), (1,)), ((), ())),
                            preferred_element_type=F32) + b_ref[0, qi - j]
        m_new = jnp.maximum(m, jnp.max(s, axis=-1, keepdims=True))
        alpha = jnp.exp(m - m_new)
        p = jnp.exp(s - m_new)
        l = alpha * l + jnp.sum(p, axis=-1, keepdims=True)
        acc = alpha * acc + _dot(p.astype(BF16), v_ref[0, 0, rows, :])
        return m_new, l, acc

    init = (jnp.full((ATTN_TILE, 1), NEG, F32), jnp.zeros((ATTN_TILE, 1), F32),
            jnp.zeros((ATTN_TILE, HEAD_DIM), F32))
    _, l, acc = lax.fori_loop(0, qi + 1, body, init)
    o_ref[0, 0] = (acc / l).astype(o_ref.dtype)


def _prompt_attention(q, k, v, rel_bias):
    b, h, s, e = q.shape
    tb = _toeplitz_bias(rel_bias, s, ATTN_TILE)
    return pl.pallas_call(
        _attn_body,
        grid=(b, h, s // ATTN_TILE),
        in_specs=[pl.BlockSpec((1, 1, ATTN_TILE, e), lambda b_, h_, i: (b_, h_, i, 0)),
                  pl.BlockSpec((1, 1, s, e), lambda b_, h_, i: (b_, h_, 0, 0)),
                  pl.BlockSpec((1, 1, s, e), lambda b_, h_, i: (b_, h_, 0, 0)),
                  pl.BlockSpec((1, s // ATTN_TILE, ATTN_TILE, ATTN_TILE), lambda b_, h_, i: (h_, 0, 0, 0))],
        out_specs=pl.BlockSpec((1, 1, ATTN_TILE, e), lambda b_, h_, i: (b_, h_, i, 0)),
        out_shape=jax.ShapeDtypeStruct((b, h, s, e), BF16),
        compiler_params=_cparams(("arbitrary", "arbitrary", "arbitrary"), 32),
        name="prompt_attention",
    )(q, k, v, tb)


def _head_segments(width, n_heads):
    seg = np.zeros((width, LANES), np.float32)
    seg[np.arange(width), np.arange(width) // HEAD_DIM] = 1.0
    assert n_heads <= LANES
    return jnp.asarray(seg), jnp.asarray(seg.T.copy())


def _sample_attn_body(q_ref, kn_ref, vn_ref, k1_ref, k4_ref, k16_ref, v1_ref, v4_ref, v16_ref,
                      bias_ref, b0_ref, seg_ref, segt_ref, o_ref):
    q = q_ref[0] * ATTN_SCALE
    seg = seg_ref[...]
    segt = segt_ref[...]
    s0 = _dot(kn_ref[0] * q, seg, HIGHEST) + b0_ref[...]
    scores = [_dot(kr[0] * q, seg, HIGHEST) + bias_ref[i] for i, kr in enumerate((k1_ref, k4_ref, k16_ref))]
    m = s0
    for s in scores:
        m = jnp.maximum(m, jnp.max(s, axis=0, keepdims=True))
    p0 = float(len(DILATIONS)) * jnp.exp(s0 - m)
    den = p0
    num = _dot(p0, segt, HIGHEST) * vn_ref[0]
    for s, vr in zip(scores, (v1_ref, v4_ref, v16_ref)):
        p = jnp.exp(s - m)
        den = den + jnp.sum(p, axis=0, keepdims=True)
        num = num + jnp.sum(_dot(p, segt, HIGHEST) * vr[0], axis=0, keepdims=True)
    o_ref[0] = (num / _dot(den, segt, HIGHEST)).astype(o_ref.dtype)


def _sample_attention(q, k_new, v_new, k_buf, v_buf, rel_bias):
    bd, _, w = q.shape
    wbuf = k_buf.shape[1]
    n = 128
    views, specs = [], []
    for buf in (k_buf, v_buf):
        for window, dil in DILATIONS:
            assert window // dil == n and wbuf % (n * dil) == 0 and window <= wbuf
            views.append(buf.reshape(bd, wbuf // dil, dil * w))
            last = wbuf // dil // n - 1
            specs.append(pl.BlockSpec((1, n, w), lambda b_, last=last: (b_, last, 0)))
    biases = []
    for window, dil in DILATIONS:
        dist = jnp.asarray(dil * (n - np.arange(n)), jnp.int32)
        biases.append(rel_bias.astype(F32)[_rel_bucket(dist)])
    bias = jnp.pad(jnp.stack(biases), ((0, 0), (0, 0), (0, LANES - H_ATTN)))
    b0 = jnp.pad(rel_bias.astype(F32)[_rel_bucket(jnp.zeros((1,), jnp.int32))], ((0, 0), (0, LANES - H_ATTN)))
    seg, segt = _head_segments(w, H_ATTN)
    row = pl.BlockSpec((1, 1, w), lambda b_: (b_, 0, 0))
    return pl.pallas_call(
        _sample_attn_body,
        grid=(bd,),
        in_specs=[row, row, row] + specs + [
            pl.BlockSpec((len(DILATIONS), n, LANES), lambda b_: (0, 0, 0)),
            pl.BlockSpec((1, LANES), lambda b_: (0, 0)),
            pl.BlockSpec((w, LANES), lambda b_: (0, 0)),
            pl.BlockSpec((LANES, w), lambda b_: (0, 0))],
        out_specs=row,
        out_shape=jax.ShapeDtypeStruct((bd, 1, w), BF16),
        compiler_params=_cparams(("arbitrary",), 40),
        name="sample_attention",
    )(q, k_new, v_new, *views, bias, b0, seg, segt)


def _softplus(z):
    return jnp.maximum(z, 0.0) + jnp.log1p(jnp.exp(-jnp.abs(z)))


def _rwkv_prep_body(r_ref, k_ref, v_ref, t_ref, rp_ref, kp_ref, vp_ref, tp_ref,
                    s0r_ref, s0k_ref, s0v_ref, s0t_ref, mur_ref, muk_ref, muv_ref, mut_ref,
                    w0_ref, a0_ref, kk_ref, ka_ref, w2_ref, a2_ref, g2_ref, seg_ref, segt_ref,
                    r_o, w_o, k_o, v_o, nkk_o, kka_o, g_o, *, single_step):
    first = pl.program_id(1) == 0

    def mix(cur_ref, prev_ref, s0_ref, mu_ref):
        cur = cur_ref[0]
        if single_step:
            shifted = s0_ref[0]
        else:
            prev_row = jnp.where(first, s0_ref[0], prev_ref[0, 7:8, :])
            row = lax.broadcasted_iota(jnp.int32, cur.shape, 0)
            shifted = jnp.where(row == 0, prev_row, pltpu.roll(cur, 1, 0))
        return cur + mu_ref[...] * (shifted - cur)

    r = mix(r_ref, rp_ref, s0r_ref, mur_ref)
    k = mix(k_ref, kp_ref, s0k_ref, muk_ref)
    v = mix(v_ref, vp_ref, s0v_ref, muv_ref)
    t = mix(t_ref, tp_ref, s0t_ref, mut_ref)
    lw = t[:, :DECAY_LORA]
    la = t[:, DECAY_LORA:DECAY_LORA + AAA_LORA]
    lg = t[:, DECAY_LORA + AAA_LORA:]
    w_log = -_softplus(-(w0_ref[...] + _dot(jnp.tanh(lw), w2_ref[...], HIGHEST))) - 0.5
    a = jax.nn.sigmoid(a0_ref[...] + _dot(la, a2_ref[...], HIGHEST))
    kk = k * kk_ref[...]
    ssq = _dot(_dot(kk * kk, seg_ref[...], HIGHEST), segt_ref[...], HIGHEST)
    kkn = kk / jnp.maximum(jnp.sqrt(ssq), 1e-12)
    r_o[...] = r
    w_o[...] = jnp.exp(-jnp.exp(w_log))
    k_o[...] = k * (1.0 + (a - 1.0) * ka_ref[...])
    v_o[...] = v
    nkk_o[...] = -kkn
    kka_o[...] = kkn * a
    g_o[...] = _dot(jax.nn.sigmoid(lg), g2_ref[...])


def _rwkv_prep(main3, tail3, shift0, tmix_mu, w0, w2, a0, a2, g2, k_k, k_a, tm, single_step):
    b, t, _ = main3.shape
    rw = RWKV_WIDTH
    seg, segt = _head_segments(rw, H_RWKV)
    s0 = shift0.reshape(b, -1, RWKV_COLS)
    s0_main, s0_tail = s0[..., :RWKV_MAIN], s0[..., RWKV_MAIN:]
    mu = tmix_mu.reshape(1, RWKV_COLS)
    mu_main, mu_tail = mu[:, :RWKV_MAIN], mu[:, RWKV_MAIN:]
    prev = 8 if t >= 8 else t

    def cur(c, width=rw):
        return pl.BlockSpec((1, tm, width), lambda b_, i: (b_, i, c))

    def prv(c, width=rw):
        return pl.BlockSpec((1, prev, width), lambda b_, i: (b_, jnp.maximum(i * (tm // prev) - 1, 0), c))

    def s0s(c, width=rw):
        if single_step:
            return cur(c, width)
        return pl.BlockSpec((1, 1, width), lambda b_, i: (b_, 0, c))

    def par(c, rows=1, width=rw):
        return pl.BlockSpec((rows, width), lambda b_, i: (0, c))

    vec = lambda x: x.reshape(1, rw)
    out_spec = pl.BlockSpec((tm, rw), lambda b_, i: (i, b_))
    out_shape = jax.ShapeDtypeStruct((t, b * rw), F32)
    return pl.pallas_call(
        functools.partial(_rwkv_prep_body, single_step=single_step),
        grid=(b, t // tm),
        in_specs=[cur(0), cur(1), cur(2), cur(0, LORA_WIDTH),
                  prv(0), prv(1), prv(2), prv(0, LORA_WIDTH),
                  s0s(0), s0s(1), s0s(2), s0s(0, LORA_WIDTH),
                  par(0), par(1), par(2), par(0, 1, LORA_WIDTH),
                  par(0), par(0), par(0), par(0),
                  par(0, DECAY_LORA), par(0, AAA_LORA), par(0, GATE_LORA),
                  pl.BlockSpec((rw, LANES), lambda b_, i: (0, 0)),
                  pl.BlockSpec((LANES, rw), lambda b_, i: (0, 0))],
        out_specs=[out_spec] * 7,
        out_shape=[out_shape] * 7,
        compiler_params=_cparams(("arbitrary", "arbitrary"), 56),
        name="rwkv_prep",
    )(main3, main3, main3, tail3, main3, main3, main3, tail3,
      s0_main, s0_main, s0_main, s0_tail, mu_main, mu_main, mu_main, mu_tail,
      vec(w0), vec(a0), vec(k_k), vec(k_a), w2, a2, g2, seg, segt)


def _rwkv_scan_body(nkk_ref, w_ref, kka_ref, k_ref, r_ref, v_ref, s0_ref, y_ref, sf_ref, s_ref, kt_ref, yt_ref,
                    *, tt, n_heads):
    first_head = pl.program_id(0) * LANES
    tb = pl.program_id(1)

    @pl.when(tb == 0)
    def _():
        s_ref[...] = s0_ref[...]

    def to_lanes(x):
        xt = x.T
        if n_heads % LANES:
            lane = lax.broadcasted_iota(jnp.int32, xt.shape, 1)
            xt = jnp.where(lane < n_heads - first_head, xt, 0.0)
        return xt

    def step(t, _):
        for j, ref in enumerate((nkk_ref, w_ref, kka_ref, k_ref, r_ref, v_ref)):
            kt_ref[j] = to_lanes(ref[t])

        def rows8(vc, _):
            ys = []
            for u in range(8):
                vi = vc * 8 + u
                s_old = s_ref[vi]
                sa = jnp.sum(s_old * kt_ref[0], axis=0, keepdims=True)
                s_new = s_old * kt_ref[1] + sa * kt_ref[2] + kt_ref[5, pl.ds(vi, 1), :] * kt_ref[3]
                s_ref[vi] = s_new
                ys.append(jnp.sum(s_new * kt_ref[4], axis=0, keepdims=True))
            yt_ref[pl.ds(pl.multiple_of(vc * 8, 8), 8), :] = jnp.concatenate(ys, axis=0)
            return 0

        lax.fori_loop(0, HEAD_DIM // 8, rows8, 0)
        y_ref[t] = yt_ref[...].T
        return 0

    lax.fori_loop(0, tt, step, 0)

    @pl.when(tb == pl.num_programs(1) - 1)
    def _():
        sf_ref[...] = s_ref[...]


def _rwkv_scan(nkk, w, kka, k, r, v, s0, tt):
    t, n_heads, e = r.shape
    groups = s0.shape[2] // LANES
    seq = pl.BlockSpec((tt, LANES, e), lambda g, i: (i, g, 0))
    st = pl.BlockSpec((e, e, LANES), lambda g, i: (0, 0, g))
    return pl.pallas_call(
        functools.partial(_rwkv_scan_body, tt=tt, n_heads=n_heads),
        grid=(groups, t // tt),
        in_specs=[seq] * 6 + [st],
        out_specs=[seq, st],
        out_shape=[jax.ShapeDtypeStruct((t, n_heads, e), F32), jax.ShapeDtypeStruct((e, e, groups * LANES), F32)],
        scratch_shapes=[pltpu.VMEM((e, e, LANES), F32), pltpu.VMEM((6, e, LANES), F32), pltpu.VMEM((e, LANES), F32)],
        compiler_params=_cparams(("arbitrary", "arbitrary"), 48),
        name="rwkv_scan",
    )(nkk, w, kka, k, r, v, s0)


def _rwkv_post_body(y_ref, r_ref, k_ref, v_ref, g_ref, lnw_ref, lnb_ref, rk_ref, seg_ref, segt_ref, o_ref):
    seg = seg_ref[...]
    segt = segt_ref[...]
    head_sum = lambda x: _dot(_dot(x, seg, HIGHEST), segt, HIGHEST)
    y = y_ref[...]
    yc = y - head_sum(y) * (1.0 / HEAD_DIM)
    var = head_sum(yc * yc) * (1.0 / HEAD_DIM)
    yn = yc * lax.rsqrt(var + GN_EPS) * lnw_ref[...] + lnb_ref[...]
    bonus = head_sum(r_ref[...] * k_ref[...] * rk_ref[...]) * v_ref[...]
    o_ref[0] = ((yn + bonus) * g_ref[...]).astype(o_ref.dtype)


def _rwkv_post(y, r, k, v, g, ln_w, ln_b, r_k, b, tm):
    t = y.shape[0]
    rw = RWKV_WIDTH
    seg, segt = _head_segments(rw, H_RWKV)
    blk = pl.BlockSpec((tm, rw), lambda b_, i: (i, b_))
    par = pl.BlockSpec((1, rw), lambda b_, i: (0, 0))
    return pl.pallas_call(
        _rwkv_post_body,
        grid=(b, t // tm),
        in_specs=[blk] * 5 + [par] * 3 + [pl.BlockSpec((rw, LANES), lambda b_, i: (0, 0)),
                                          pl.BlockSpec((LANES, rw), lambda b_, i: (0, 0))],
        out_specs=pl.BlockSpec((1, tm, rw), lambda b_, i: (b_, i, 0)),
        out_shape=jax.ShapeDtypeStruct((b, t, rw), BF16),
        compiler_params=_cparams(("arbitrary", "arbitrary"), 40),
        name="rwkv_post",
    )(y, r, k, v, g, ln_w.reshape(1, rw), ln_b.reshape(1, rw), r_k.reshape(1, rw), seg, segt)


def _rwkv_time_mix(main3, tail3, shift0, state0, tmix_mu, w0, w2, a0, a2, g2, k_k, k_a, r_k, ln_w, ln_b,
                   tm, tt, single_step):
    b, t, _ = main3.shape
    r, w, k, v, nkk, kka, g = _rwkv_prep(main3, tail3, shift0, tmix_mu, w0, w2, a0, a2, g2, k_k, k_a, tm, single_step)
    nseq, steps = (t, 1) if single_step else (b, t)
    n = nseq * H_RWKV
    lanes = -(-n // LANES) * LANES
    s0 = state0.astype(F32).reshape(n, HEAD_DIM, HEAD_DIM).transpose(1, 2, 0)
    s0 = jnp.pad(s0, ((0, 0), (0, 0), (0, lanes - n)))
    per_head = lambda x: x.reshape(steps, n, HEAD_DIM)
    ys, sf = _rwkv_scan(*(per_head(x) for x in (nkk, w, kka, k, r, v)), s0, tt)
    state = sf[:, :, :n].transpose(2, 0, 1).reshape(nseq, H_RWKV, HEAD_DIM, HEAD_DIM)
    return _rwkv_post(ys.reshape(t, b * RWKV_WIDTH), r, k, v, g, ln_w, ln_b, r_k, b, tm), state


def _moe_up_body(be_ref, x_ref, wg_ref, wu_ref, h_ref):
    lo, hi = _unpack_bf16_pairs(_load_slabs(x_ref))
    x = jnp.concatenate([lo, hi], axis=1).astype(BF16)
    g = _dot(x, wg_ref[0])
    h_ref[...] = (g * jax.nn.sigmoid(g) * _dot(x, wu_ref[0])).astype(h_ref.dtype)


def _moe_down_body(be_ref, h_ref, wd_ref, y_ref):
    _store_slabs(y_ref, _pack_bf16_pairs(_dot(h_ref[...], wd_ref[0])))


def _grouped_mlp(xs, blk_expert, w_gate, w_up, w_down, tf=512):
    rows = xs.shape[0]
    nblk = rows // MOE_ROWS
    _, d, ff = w_gate.shape
    slab = lambda: pl.BlockSpec((MOE_ROWS, SLAB_ROWS, LANES), lambda *a: (a[-2], 0, 0))
    h = pl.pallas_call(
        _moe_up_body,
        grid_spec=pltpu.PrefetchScalarGridSpec(
            num_scalar_prefetch=1,
            grid=(ff // tf, nblk),
            in_specs=[slab(),
                      pl.BlockSpec((1, d, tf), lambda f, m, be: (be[m], 0, f)),
                      pl.BlockSpec((1, d, tf), lambda f, m, be: (be[m], 0, f))],
            out_specs=pl.BlockSpec((MOE_ROWS, tf), lambda f, m, be: (m, f))),
        out_shape=jax.ShapeDtypeStruct((rows, ff), BF16),
        compiler_params=_cparams(("arbitrary", "arbitrary"), 48),
        name="moe_up",
    )(blk_expert, xs, w_gate, w_up)
    return pl.pallas_call(
        _moe_down_body,
        grid_spec=pltpu.PrefetchScalarGridSpec(
            num_scalar_prefetch=1,
            grid=(nblk,),
            in_specs=[pl.BlockSpec((MOE_ROWS, ff), lambda m, be: (m, 0)),
                      pl.BlockSpec((1, ff, d), lambda m, be: (be[m], 0, 0))],
            out_specs=slab()),
        out_shape=jax.ShapeDtypeStruct((rows, SLAB_ROWS, LANES), jnp.uint32),
        compiler_params=_cparams(("arbitrary",), 48),
        name="moe_down",
    )(blk_expert, h, w_down)


def _dispatch_plan(eidx, n_blocks):
    onehot = jnp.sum((eidx[:, :, None] == jnp.arange(N_EXPERTS, dtype=jnp.int32)).astype(jnp.int32), axis=1)
    before = jnp.cumsum(onehot, axis=0) - onehot
    rank = jnp.take_along_axis(before, eidx, axis=1)
    counts = jnp.sum(onehot, axis=0)
    padded = (counts + MOE_ROWS - 1) // MOE_ROWS * MOE_ROWS
    pad_end = jnp.cumsum(padded)
    dest = (pad_end - padded)[eidx] + rank
    blk_start = jnp.arange(n_blocks, dtype=jnp.int32) * MOE_ROWS
    blk_expert = jnp.minimum(jnp.searchsorted(pad_end, blk_start, side="right"), N_EXPERTS - 1).astype(jnp.int32)
    return dest.astype(jnp.int32), blk_expert


def _token_tiles(a, tm):
    n = a.shape[0]
    n_tiles = -(-n // tm)
    a = jnp.pad(a, ((0, n_tiles * tm - n), (0, 0)))
    return a.reshape(n_tiles, tm, TOP_K).transpose(0, 2, 1)


def _dispatch_body(dest_hbm, h_hbm, xs_in_hbm, xs_hbm, dest_s, idx_sem, sem, *, tm, n_tok):
    del xs_in_hbm
    base = pl.program_id(0) * tm
    table = pltpu.make_async_copy(dest_hbm.at[pl.program_id(0)], dest_s, idx_sem)
    table.start()
    table.wait()

    def copies(r):
        return [pltpu.make_async_copy(h_hbm.at[base + r], xs_hbm.at[dest_s[k, r]], sem) for k in range(TOP_K)]

    def issue(r, _):
        @pl.when(base + r < n_tok)
        def _():
            for c in copies(r):
                c.start()
        return 0

    def drain(r, _):
        @pl.when(base + r < n_tok)
        def _():
            for c in copies(r):
                c.wait()
        return 0

    lax.fori_loop(0, tm, issue, 0)
    lax.fori_loop(0, tm, drain, 0)


def _dispatch(h_slabs, dest, n_rows, tm=128):
    n_tok = dest.shape[0]
    tiles = _token_tiles(dest, tm)
    any_spec = pl.BlockSpec(memory_space=pl.ANY)
    return pl.pallas_call(
        functools.partial(_dispatch_body, tm=tm, n_tok=n_tok),
        grid=(tiles.shape[0],),
        in_specs=[any_spec, any_spec, any_spec],
        out_specs=any_spec,
        out_shape=jax.ShapeDtypeStruct((n_rows, SLAB_ROWS, LANES), jnp.uint32),
        scratch_shapes=[pltpu.SMEM((TOP_K, tm), jnp.int32), pltpu.SemaphoreType.DMA(()), pltpu.SemaphoreType.DMA(())],
        input_output_aliases={2: 0},
        compiler_params=_cparams(("arbitrary",)),
        name="moe_dispatch",
    )(tiles, h_slabs, jnp.zeros((n_rows, SLAB_ROWS, LANES), jnp.uint32))


def _final_body(dest_hbm, gate_hbm, x_ref, shared_ref, gt_ref, fn_ref, yb_hbm, o_ref,
                dest_s, gate_s, gbuf, routed, idx_sem, sem, *, tm):
    i = pl.program_id(0)
    tables = [pltpu.make_async_copy(dest_hbm.at[i], dest_s, idx_sem.at[0]),
              pltpu.make_async_copy(gate_hbm.at[i], gate_s, idx_sem.at[1])]
    for c in tables:
        c.start()
    for c in tables:
        c.wait()

    def copies(r):
        return [pltpu.make_async_copy(yb_hbm.at[dest_s[k, r]], gbuf.at[r * TOP_K + k], sem) for k in range(TOP_K)]

    def issue(r, _):
        for c in copies(r):
            c.start()
        return 0

    def drain(r, _):
        for c in copies(r):
            c.wait()
        return 0

    lax.fori_loop(0, tm, issue, 0)
    lax.fori_loop(0, tm, drain, 0)

    def combine(r, _):
        lo = jnp.zeros((SLAB_ROWS, LANES), F32)
        hi = jnp.zeros((SLAB_ROWS, LANES), F32)
        for k in range(TOP_K):
            g = gate_s[k, r]
            l, h = _unpack_bf16_pairs(gbuf[r * TOP_K + k])
            lo = lo + g * l
            hi = hi + g * h
        routed[r, :SLAB_ROWS, :] = lo
        routed[r, SLAB_ROWS:, :] = hi
        return 0

    lax.fori_loop(0, tm, combine, 0)
    s_lo, s_hi = _unpack_bf16_pairs(shared_ref[...])
    x = x_ref[...] + gt_ref[...] * (routed[...] + jnp.concatenate([s_lo, s_hi], axis=1))
    ms = jnp.sum(jnp.sum(x * x, axis=2, keepdims=True), axis=1, keepdims=True) * (1.0 / D_MODEL)
    o_ref[...] = x * lax.rsqrt(ms + RMS_EPS) * fn_ref[...]


def _final(x_slabs, shared_slabs, shared_tile0, yb, dest, gate, mod_slabs, gt_tiles_per_row, final_norm, tm):
    n = x_slabs.shape[0]
    rows = 2 * SLAB_ROWS
    any_spec = pl.BlockSpec(memory_space=pl.ANY)
    tok = pl.BlockSpec((tm, rows, LANES), lambda i: (i, 0, 0))
    if gt_tiles_per_row:
        gt_spec = pl.BlockSpec((1, rows, LANES), lambda i: (i // gt_tiles_per_row, 5, 0))
    else:
        gt_spec = pl.BlockSpec((tm, rows, LANES), lambda i: (i, 5, 0))
    return pl.pallas_call(
        functools.partial(_final_body, tm=tm),
        grid=(n // tm,),
        in_specs=[any_spec, any_spec, tok,
                  pl.BlockSpec((tm, SLAB_ROWS, LANES), lambda i: (i + shared_tile0, 0, 0)),
                  gt_spec, pl.BlockSpec((1, rows, LANES), lambda i: (0, 0, 0)), any_spec],
        out_specs=tok,
        out_shape=jax.ShapeDtypeStruct((n, rows, LANES), F32),
        scratch_shapes=[pltpu.SMEM((TOP_K, tm), jnp.int32), pltpu.SMEM((TOP_K, tm), F32),
                        pltpu.VMEM((tm * TOP_K, SLAB_ROWS, LANES), jnp.uint32),
                        pltpu.VMEM((tm, rows, LANES), F32),
                        pltpu.SemaphoreType.DMA((2,)), pltpu.SemaphoreType.DMA(())],
        compiler_params=_cparams(("arbitrary",), 40),
        name="final",
    )(_token_tiles(dest, tm), _token_tiles(gate, tm), x_slabs, shared_slabs, mod_slabs,
      final_norm.reshape(1, rows, LANES), yb)


def _project(x3, mod3, norm1, w_in, tm_norm, tm):
    b, t, d = x3.shape
    h = _norm_mod(x3, norm1, mod3, 1, 0, tm_norm).reshape(b * t, d)
    tn = 512
    q, k, v = (_matmul(h, w_in, i * ATTN_WIDTH, ATTN_WIDTH, tm, tn) for i in range(3))
    main = _matmul(h, w_in, ATTN_COLS, RWKV_MAIN, tm, tn)
    tail = _matmul(h, w_in[:, ATTN_COLS + RWKV_MAIN:], 0, LORA_WIDTH, tm, LORA_WIDTH)
    return q, k, v, main, tail


def kernel(x_prompt, x_sample, cache_attn_k, cache_attn_v, state_rwkv, state_shift, c_prompt, c_sample,
           rel_bias, norm1, norm2, final_norm, w_ada, b_ada, w_in, w_out, tmix_mu, w0, w2, a0, a2, g2,
           k_k, k_a, r_k, ln_w, ln_b, router_w, router_b, ws_gate, ws_up, ws_down, w_gate, w_up, w_down):
    bp, sp, d = x_prompt.shape
    bs = x_sample.shape[0]
    assert w_in.shape[0] == 1
    l = 0

    c_all = jnp.concatenate([c_prompt, c_sample, jnp.zeros((4, d), F32)], axis=0)
    mod = _ada_mod(c_all, w_ada[l], b_ada[l])
    mod_p = mod[:bp].reshape(bp, 1, 6 * d)
    mod_s = mod[bp:bp + bs].reshape(1, bs, 6 * d)
    xs3 = x_sample.reshape(1, bs, d)
    tm_p, tm_s, tm_norm = 512, bs, 256

    rwkv_par = (tmix_mu[l], w0[l], w2[l], a0[l], a2[l], g2[l], k_k[l], k_a[l], r_k[l], ln_w[l], ln_b[l])

    qp, kp, vp, main_p, tail_p = _project(x_prompt, mod_p, norm1[l], w_in[l], tm_norm, tm_p)
    heads = lambda a: a.reshape(bp, sp, H_ATTN, HEAD_DIM).transpose(0, 2, 1, 3).astype(BF16)
    o_attn_p = _prompt_attention(heads(qp), heads(kp), heads(vp), rel_bias)
    o_attn_p = o_attn_p.transpose(0, 2, 1, 3).reshape(bp, sp, ATTN_WIDTH)
    main_p3 = main_p.reshape(bp, sp, RWKV_MAIN)
    tail_p3 = tail_p.reshape(bp, sp, LORA_WIDTH)
    o_rwkv_p, state_p = _rwkv_time_mix(main_p3, tail_p3, jnp.zeros((bp, RWKV_COLS), F32),
                                       jnp.zeros((bp, H_RWKV, HEAD_DIM, HEAD_DIM), F32), *rwkv_par,
                                       tm=128, tt=32, single_step=False)
    shift_p = jnp.concatenate([main_p3[:, -1], tail_p3[:, -1]], axis=-1)
    x1_p = _matmul_gated_residual(jnp.concatenate([o_attn_p, o_rwkv_p], axis=-1), w_out[l], x_prompt, mod_p, 2, tm_p)
    router = (router_w[l], router_b[l])
    h2_p, e_p, g_p = _norm_mod(x1_p, norm2[l], mod_p, 4, 3, tm_norm, router)

    qs, ks, vs, main_s, tail_s = _project(xs3, mod_s, norm1[l], w_in[l], tm_s, tm_s)
    row3 = lambda a: a.reshape(bs, 1, a.shape[-1])
    o_attn_s = _sample_attention(row3(qs), row3(ks), row3(vs),
                                 cache_attn_k[l].reshape(bs, -1, ATTN_WIDTH),
                                 cache_attn_v[l].reshape(bs, -1, ATTN_WIDTH), rel_bias)
    o_rwkv_s, state_s = _rwkv_time_mix(main_s.reshape(1, bs, RWKV_MAIN), tail_s.reshape(1, bs, LORA_WIDTH),
                                       state_shift[l].reshape(1, bs, RWKV_COLS), state_rwkv[l], *rwkv_par,
                                       tm=bs, tt=1, single_step=True)
    shift_s = jnp.concatenate([main_s, tail_s], axis=-1)
    o_cat_s = jnp.concatenate([o_attn_s.reshape(1, bs, ATTN_WIDTH), o_rwkv_s], axis=-1)
    x1_s = _matmul_gated_residual(o_cat_s, w_out[l], xs3, mod_s, 2, tm_s)
    h2_s, e_s, g_s = _norm_mod(x1_s, norm2[l], mod_s, 4, 3, tm_s, router)

    n_p, n_s = bp * sp, bs
    n_all = n_p + n_s
    n_pad = -(-n_all // MOE_ROWS) * MOE_ROWS
    h_slabs = jnp.concatenate([h2_p, h2_s, jnp.zeros((n_pad - n_all, SLAB_ROWS, LANES), jnp.uint32)], axis=0)
    top = lambda a, b: jnp.concatenate([a.reshape(n_p, LANES)[:, :TOP_K], b.reshape(n_s, LANES)[:, :TOP_K]], axis=0)
    eidx, gate = top(e_p, e_s), top(g_p, g_s)
    n_blocks = -(-(n_all * TOP_K) // MOE_ROWS) + N_EXPERTS
    dest, blk_expert = _dispatch_plan(eidx, n_blocks)
    yb = _grouped_mlp(_dispatch(h_slabs, dest, n_blocks * MOE_ROWS), blk_expert, w_gate[l], w_up[l], w_down[l])
    shared = _grouped_mlp(h_slabs, jnp.zeros((n_pad // MOE_ROWS,), jnp.int32), ws_gate, ws_up, ws_down)

    rows = 2 * SLAB_ROWS
    tm_f = 128
    y_p = _final(x1_p.reshape(n_p, rows, LANES), shared, 0, yb, dest[:n_p], gate[:n_p],
                 mod_p.reshape(bp, 6 * rows, LANES), sp // tm_f, final_norm, tm_f)
    y_s = _final(x1_s.reshape(n_s, rows, LANES), shared, n_p // n_s, yb, dest[n_p:], gate[n_p:],
                 mod_s.reshape(bs, 6 * rows, LANES), 0, final_norm, n_s)
    y_p = y_p.reshape(bp, sp, d)

    return (y_p, y_s.reshape(bs, 1, d),
            kp.reshape(1, bp, sp, H_ATTN, HEAD_DIM), vp.reshape(1, bp, sp, H_ATTN, HEAD_DIM),
            state_p[None], shift_p[None],
            ks.reshape(1, bs, 1, H_ATTN, HEAD_DIM), vs.reshape(1, bs, 1, H_ATTN, HEAD_DIM),
            state_s[None], shift_s[None])
```

```python
import functools
import math

import numpy as np
import jax
import jax.numpy as jnp
from jax import lax
from jax.experimental import pallas as pl
from jax.experimental.pallas import tpu as pltpu

F32 = jnp.float32
BF16 = jnp.bfloat16
HIGHEST = lax.Precision.HIGHEST

D_MODEL = 4096
HEAD_DIM = 64
H_ATTN = 24
ATTN_WIDTH = H_ATTN * HEAD_DIM
H_RWKV = 40
RWKV_WIDTH = H_RWKV * HEAD_DIM
DILATIONS = ((128, 1), (512, 4), (2048, 16))
NUM_BUCKETS = 32
MAX_DISTANCE = 2048
DECAY_LORA = 128
AAA_LORA = 128
GATE_LORA = 480
LORA_WIDTH = DECAY_LORA + AAA_LORA + GATE_LORA
ATTN_COLS = 3 * ATTN_WIDTH
RWKV_MAIN = 3 * RWKV_WIDTH
RWKV_COLS = RWKV_MAIN + LORA_WIDTH
N_EXPERTS = 256
TOP_K = 8
N_GROUPS = 8
TOPK_GROUPS = 4
EXPERT_FF = 1024
ROUTED_SCALE = 2.5
RMS_EPS = 1e-6
GN_EPS = 64e-5
ATTN_SCALE = HEAD_DIM ** -0.5
NEG = -1e30

LANES = 128
SLAB_ROWS = D_MODEL // 2 // LANES
MOE_ROWS = 128
ATTN_TILE = 256
MIB = 1 << 20


def _cparams(dims, vmem_mib=None):
    kw = dict(dimension_semantics=dims)
    if vmem_mib is not None:
        kw["vmem_limit_bytes"] = vmem_mib * MIB
    return pltpu.CompilerParams(**kw)


def _dot(a, b, precision=None):
    return lax.dot_general(a, b, (((a.ndim - 1,), (0,)), ((), ())),
                           precision=precision, preferred_element_type=F32)


def _ada_body(c_ref, w_ref, b_ref, o_ref):
    c = c_ref[...]
    o_ref[...] = _dot(c * jax.nn.sigmoid(c), w_ref[...]) + b_ref[...]


def _ada_mod(c_all, w_ada, b_ada, tn=512):
    m, k = c_all.shape
    n = w_ada.shape[1]
    return pl.pallas_call(
        _ada_body,
        grid=(n // tn,),
        in_specs=[pl.BlockSpec((m, k), lambda j: (0, 0)),
                  pl.BlockSpec((k, tn), lambda j: (0, j)),
                  pl.BlockSpec((1, tn), lambda j: (0, j))],
        out_specs=pl.BlockSpec((m, tn), lambda j: (0, j)),
        out_shape=jax.ShapeDtypeStruct((m, n), F32),
        compiler_params=_cparams(("arbitrary",), 40),
        name="ada_mod",
    )(c_all, w_ada, b_ada.reshape(1, n))


def _norm_mod_body(x_ref, g_ref, sc_ref, sh_ref, o_ref):
    x = x_ref[0]
    y = x * lax.rsqrt(jnp.mean(x * x, axis=-1, keepdims=True) + RMS_EPS) * g_ref[...]
    o_ref[0] = (y * (1.0 + sc_ref[0]) + sh_ref[0]).astype(o_ref.dtype)


def _pack_bf16_pairs(h):
    bits = lax.bitcast_convert_type(h.astype(BF16).astype(F32), jnp.uint32)
    half = h.shape[1] // 2
    return (bits[:, half:] & jnp.uint32(0xFFFF0000)) | (bits[:, :half] >> 16)


def _unpack_bf16_pairs(w):
    return (lax.bitcast_convert_type(w << 16, F32),
            lax.bitcast_convert_type(w & jnp.uint32(0xFFFF0000), F32))


def _store_slabs(ref, words):
    for j in range(SLAB_ROWS):
        ref[:, j, :] = words[:, j * LANES:(j + 1) * LANES]


def _load_slabs(ref):
    return jnp.concatenate([ref[:, j, :] for j in range(SLAB_ROWS)], axis=1)


def _top_k_route(scores, router_b):
    m = scores.shape[0]
    gsz = N_EXPERTS // N_GROUPS
    lane = lax.broadcasted_iota(jnp.int32, scores.shape, 1).astype(F32)
    grp = jnp.floor(lane * (1.0 / gsz))
    choice = scores + router_b
    ninf = -jnp.inf
    row_max = lambda x: jnp.max(x, axis=-1, keepdims=True)
    row_min = lambda x: jnp.min(x, axis=-1, keepdims=True)
    gscore = jnp.zeros_like(choice)
    for g in range(N_GROUPS):
        ing = grp == float(g)
        cg = jnp.where(ing, choice, ninf)
        m1 = row_max(cg)
        i1 = row_min(jnp.where(cg == m1, lane, float(N_EXPERTS)))
        m2 = row_max(jnp.where(lane == i1, ninf, cg))
        gscore = jnp.where(ing, m1 + m2, gscore)
    cand = jnp.full_like(choice, ninf)
    for _ in range(TOPK_GROUPS):
        best = row_max(gscore)
        gsel = row_min(jnp.where(gscore == best, grp, float(N_GROUPS)))
        hit = grp == gsel
        cand = jnp.where(hit, choice, cand)
        gscore = jnp.where(hit, ninf, gscore)
    out_lane = lax.broadcasted_iota(jnp.int32, (m, LANES), 1)
    ids = jnp.zeros((m, LANES), F32)
    sel = jnp.zeros((m, LANES), F32)
    for k in range(TOP_K):
        best = row_max(cand)
        idx = row_min(jnp.where(cand == best, lane, float(N_EXPERTS)))
        hit = lane == idx
        ids = jnp.where(out_lane == k, idx, ids)
        sel = jnp.where(out_lane == k, jnp.sum(jnp.where(hit, scores, 0.0), axis=-1, keepdims=True), sel)
        cand = jnp.where(hit, ninf, cand)
    return ids, sel / jnp.sum(sel, axis=-1, keepdims=True) * ROUTED_SCALE


def _norm_mod_router_body(x_ref, g_ref, sc_ref, sh_ref, rw_ref, rb_ref, o_ref, e_ref, w_ref):
    x = x_ref[0]
    y = x * lax.rsqrt(jnp.mean(x * x, axis=-1, keepdims=True) + RMS_EPS) * g_ref[...]
    h = y * (1.0 + sc_ref[0]) + sh_ref[0]
    _store_slabs(o_ref, _pack_bf16_pairs(h))
    ids, gate = _top_k_route(jax.nn.sigmoid(_dot(h, rw_ref[...])), rb_ref[...])
    e_ref[0] = ids.astype(jnp.int32)
    w_ref[0] = gate


def _mod_spec(mod3, chunk, tm):
    if mod3.shape[1] == 1:
        return pl.BlockSpec((1, 1, D_MODEL), lambda b, i: (b, 0, chunk))
    return pl.BlockSpec((1, tm, D_MODEL), lambda b, i: (b, i, chunk))


def _norm_mod(x3, g, mod3, sc_chunk, sh_chunk, tm, router=None):
    b, t, d = x3.shape
    in_specs = [pl.BlockSpec((1, tm, d), lambda b_, i: (b_, i, 0)),
                pl.BlockSpec((1, d), lambda b_, i: (0, 0)),
                _mod_spec(mod3, sc_chunk, tm),
                _mod_spec(mod3, sh_chunk, tm)]
    h_spec = pl.BlockSpec((1, tm, d), lambda b_, i: (b_, i, 0))
    h_shape = jax.ShapeDtypeStruct((b, t, d), BF16)
    if router is None:
        return pl.pallas_call(
            _norm_mod_body, grid=(b, t // tm), in_specs=in_specs, out_specs=h_spec, out_shape=h_shape,
            compiler_params=_cparams(("arbitrary", "arbitrary"), 40), name="norm_mod",
        )(x3, g.reshape(1, d), mod3, mod3)
    router_w, router_b = router
    nt = t // tm
    in_specs += [pl.BlockSpec((d, N_EXPERTS), lambda b_, i: (0, 0)),
                 pl.BlockSpec((1, N_EXPERTS), lambda b_, i: (0, 0))]
    sel_spec = pl.BlockSpec((1, tm, LANES), lambda b_, i: (b_, i, 0))
    return pl.pallas_call(
        _norm_mod_router_body, grid=(b, nt), in_specs=in_specs,
        out_specs=[pl.BlockSpec((tm, SLAB_ROWS, LANES), lambda b_, i: (b_ * nt + i, 0, 0)), sel_spec, sel_spec],
        out_shape=[jax.ShapeDtypeStruct((b * t, SLAB_ROWS, LANES), jnp.uint32),
                   jax.ShapeDtypeStruct((b, t, LANES), jnp.int32),
                   jax.ShapeDtypeStruct((b, t, LANES), F32)],
        compiler_params=_cparams(("arbitrary", "arbitrary"), 48), name="norm_mod_router",
    )(x3, g.reshape(1, d), mod3, mod3, router_w, router_b.reshape(1, N_EXPERTS))


def _mm_body(x_ref, w_ref, o_ref):
    o_ref[...] = _dot(x_ref[...], w_ref[...]).astype(o_ref.dtype)


def _matmul(x, w, col0, n_cols, tm, tn, out_dtype=F32):
    m, k = x.shape
    assert col0 % tn == 0 and n_cols % tn == 0 and m % tm == 0
    off = col0 // tn
    return pl.pallas_call(
        _mm_body,
        grid=(n_cols // tn, m // tm),
        in_specs=[pl.BlockSpec((tm, k), lambda j, i: (i, 0)),
                  pl.BlockSpec((k, tn), lambda j, i: (0, j + off))],
        out_specs=pl.BlockSpec((tm, tn), lambda j, i: (i, j)),
        out_shape=jax.ShapeDtypeStruct((m, n_cols), out_dtype),
        compiler_params=_cparams(("arbitrary", "arbitrary"), 48),
        name="matmul",
    )(x, w)


def _mm_res_body(x_ref, w_ref, res_ref, g_ref, o_ref):
    o_ref[0] = res_ref[0] + g_ref[0] * _dot(x_ref[0], w_ref[...])


def _matmul_gated_residual(x3, w, res3, mod3, gate_chunk, tm, tn=512):
    b, t, k = x3.shape
    n = w.shape[1]
    nt = n // tn
    if mod3.shape[1] == 1:
        g_spec = pl.BlockSpec((1, 1, tn), lambda j, b_, i: (b_, 0, gate_chunk * nt + j))
    else:
        g_spec = pl.BlockSpec((1, tm, tn), lambda j, b_, i: (b_, i, gate_chunk * nt + j))
    return pl.pallas_call(
        _mm_res_body,
        grid=(nt, b, t // tm),
        in_specs=[pl.BlockSpec((1, tm, k), lambda j, b_, i: (b_, i, 0)),
                  pl.BlockSpec((k, tn), lambda j, b_, i: (0, j)),
                  pl.BlockSpec((1, tm, tn), lambda j, b_, i: (b_, i, j)),
                  g_spec],
        out_specs=pl.BlockSpec((1, tm, tn), lambda j, b_, i: (b_, i, j)),
        out_shape=jax.ShapeDtypeStruct((b, t, n), F32),
        compiler_params=_cparams(("arbitrary", "arbitrary", "arbitrary"), 48),
        name="matmul_gated_residual",
    )(x3, w, res3, mod3)


def _rel_bucket(dist):
    n_exact = NUM_BUCKETS // 2
    df = jnp.maximum(dist, 1).astype(F32)
    large = n_exact + (jnp.log(df / n_exact) / math.log(MAX_DISTANCE / n_exact)
                       * (NUM_BUCKETS - n_exact)).astype(jnp.int32)
    return jnp.where(dist < n_exact, dist, jnp.minimum(large, NUM_BUCKETS - 1))


def _distance_bias(rel_bias, seq):
    dist = np.arange(seq)
    mult = np.zeros(seq, np.float64)
    for window, dil in DILATIONS:
        mult += (dist % dil == 0) & (dist <= window)
    logm = jnp.asarray(np.log(np.maximum(mult, 1.0)), F32)
    bias = rel_bias.astype(F32)[_rel_bucket(jnp.asarray(dist, jnp.int32))] + logm[:, None]
    bias = jnp.where(jnp.asarray(mult > 0)[:, None], bias, NEG)
    return bias.T


def _toeplitz_body(p_ref, o_ref):
    tile = o_ref.shape[-1]
    x = jnp.broadcast_to(p_ref[0, 0], (tile, 2 * tile))
    o_ref[0, 0] = pltpu.roll(x, 0, 1, stride=1, stride_axis=0)[:, :tile]


def _toeplitz_bias(rel_bias, seq, tile):
    nd = seq // tile
    cb = _distance_bias(rel_bias, seq)
    cb_ext = jnp.concatenate([jnp.full((H_ATTN, tile), NEG, F32), cb], axis=1)
    m = np.arange(2 * tile)
    back = np.where(m < tile, -m, 2 * tile - m)
    idx = np.minimum(np.arange(nd)[:, None] * tile + back[None, :] + tile, seq + tile - 1).astype(np.int32)
    rows = jnp.take(cb_ext, jnp.asarray(idx.reshape(-1)), axis=1).reshape(H_ATTN, nd, 1, 2 * tile)
    return pl.pallas_call(
        _toeplitz_body,
        grid=(H_ATTN, nd),
        in_specs=[pl.BlockSpec((1, 1, 1, 2 * tile), lambda h, d: (h, d, 0, 0))],
        out_specs=pl.BlockSpec((1, 1, tile, tile), lambda h, d: (h, d, 0, 0)),
        out_shape=jax.ShapeDtypeStruct((H_ATTN, nd, tile, tile), F32),
        compiler_params=_cparams(("arbitrary", "arbitrary")),
        name="toeplitz_bias",
    )(rows)


def _attn_body(q_ref, k_ref, v_ref, b_ref, o_ref):
    qi = pl.program_id(2)
    q = (q_ref[0, 0].astype(F32) * ATTN_SCALE).astype(BF16)

    def body(j, carry):
        m, l, acc = carry
        rows = pl.ds(pl.multiple_of(j * ATTN_TILE, ATTN_TILE), ATTN_TILE)
        s = lax.dot_general(q, k_ref[0, 0, rows, :], (((1,), (1,)), ((), ())),
                            preferred_element_type=F32) + b_ref[0, qi - j]
        m_new = jnp.maximum(m, jnp.max(s, axis=-1, keepdims=True))
        alpha = jnp.exp(m - m_new)
        p = jnp.exp(s - m_new)
        l = alpha * l + jnp.sum(p, axis=-1, keepdims=True)
        acc = alpha * acc + _dot(p.astype(BF16), v_ref[0, 0, rows, :])
        return m_new, l, acc

    init = (jnp.full((ATTN_TILE, 1), NEG, F32), jnp.zeros((ATTN_TILE, 1), F32),
            jnp.zeros((ATTN_TILE, HEAD_DIM), F32))
    _, l, acc = lax.fori_loop(0, qi + 1, body, init)
    o_ref[0, 0] = (acc / l).astype(o_ref.dtype)


def _prompt_attention(q, k, v, rel_bias):
    b, h, s, e = q.shape
    tb = _toeplitz_bias(rel_bias, s, ATTN_TILE)
    return pl.pallas_call(
        _attn_body,
        grid=(b, h, s // ATTN_TILE),
        in_specs=[pl.BlockSpec((1, 1, ATTN_TILE, e), lambda b_, h_, i: (b_, h_, i, 0)),
                  pl.BlockSpec((1, 1, s, e), lambda b_, h_, i: (b_, h_, 0, 0)),
                  pl.BlockSpec((1, 1, s, e), lambda b_, h_, i: (b_, h_, 0, 0)),
                  pl.BlockSpec((1, s // ATTN_TILE, ATTN_TILE, ATTN_TILE), lambda b_, h_, i: (h_, 0, 0, 0))],
        out_specs=pl.BlockSpec((1, 1, ATTN_TILE, e), lambda b_, h_, i: (b_, h_, i, 0)),
        out_shape=jax.ShapeDtypeStruct((b, h, s, e), BF16),
        compiler_params=_cparams(("arbitrary", "arbitrary", "arbitrary"), 32),
        name="prompt_attention",
    )(q, k, v, tb)


def _head_segments(width, n_heads):
    seg = np.zeros((width, LANES), np.float32)
    seg[np.arange(width), np.arange(width) // HEAD_DIM] = 1.0
    assert n_heads <= LANES
    return jnp.asarray(seg), jnp.asarray(seg.T.copy())


def _sample_attn_body(q_ref, kn_ref, vn_ref, k1_ref, k4_ref, k16_ref, v1_ref, v4_ref, v16_ref,
                      bias_ref, b0_ref, o_ref):
    q = q_ref[0] * ATTN_SCALE
    s0 = jnp.sum(kn_ref[0] * q, axis=-1, keepdims=True) + b0_ref[...]
    scores = [jnp.sum(kr[0, :, 0] * q, axis=-1, keepdims=True) + bias_ref[i]
              for i, kr in enumerate((k1_ref, k4_ref, k16_ref))]
    m = s0
    for s in scores:
        m = jnp.maximum(m, jnp.max(s, axis=0))
    p0 = float(len(DILATIONS)) * jnp.exp(s0 - m)
    den = p0
    num = p0 * vn_ref[0]
    for s, vr in zip(scores, (v1_ref, v4_ref, v16_ref)):
        p = jnp.exp(s - m)
        den = den + jnp.sum(p, axis=0)
        num = num + jnp.sum(p * vr[0, :, 0], axis=0)
    o_ref[0] = num / den


def _sample_attention(q, k_new, v_new, k_buf, v_buf, rel_bias):
    bd, h, e = q.shape
    wbuf = k_buf.shape[1]
    n = 128
    views, specs = [], []
    for buf in (k_buf, v_buf):
        for window, dil in DILATIONS:
            assert window // dil == n and wbuf % (n * dil) == 0 and window <= wbuf
            views.append(buf.reshape(bd, wbuf // dil, dil, h, e))
            last = wbuf // dil // n - 1
            specs.append(pl.BlockSpec((1, n, 1, h, e), lambda b_, last=last: (b_, last, 0, 0, 0)))
    biases = []
    for window, dil in DILATIONS:
        dist = jnp.asarray(dil * (n - np.arange(n)), jnp.int32)
        biases.append(rel_bias.astype(F32)[_rel_bucket(dist)])
    bias = jnp.stack(biases)[..., None]
    b0 = rel_bias.astype(F32)[_rel_bucket(jnp.zeros((1,), jnp.int32))].reshape(h, 1)
    row = pl.BlockSpec((1, h, e), lambda b_: (b_, 0, 0))
    return pl.pallas_call(
        _sample_attn_body,
        grid=(bd,),
        in_specs=[row, row, row] + specs + [
            pl.BlockSpec((len(DILATIONS), n, h, 1), lambda b_: (0, 0, 0, 0)),
            pl.BlockSpec((h, 1), lambda b_: (0, 0))],
        out_specs=row,
        out_shape=jax.ShapeDtypeStruct((bd, h, e), F32),
        compiler_params=_cparams(("arbitrary",), 48),
        name="sample_attention",
    )(q, k_new, v_new, *views, bias, b0)


def _softplus(z):
    return jnp.maximum(z, 0.0) + jnp.log1p(jnp.exp(-jnp.abs(z)))


def _rwkv_prep_body(r_ref, k_ref, v_ref, t_ref, rp_ref, kp_ref, vp_ref, tp_ref,
                    s0r_ref, s0k_ref, s0v_ref, s0t_ref, mur_ref, muk_ref, muv_ref, mut_ref,
                    w0_ref, a0_ref, kk_ref, ka_ref, w2_ref, a2_ref, g2_ref, seg_ref, segt_ref,
                    r_o, w_o, k_o, v_o, nkk_o, kka_o, g_o, *, single_step):
    first = pl.program_id(1) == 0

    def mix(cur_ref, prev_ref, s0_ref, mu_ref):
        cur = cur_ref[0]
        if single_step:
            shifted = s0_ref[0]
        else:
            prev_row = jnp.where(first, s0_ref[0], prev_ref[0, 7:8, :])
            row = lax.broadcasted_iota(jnp.int32, cur.shape, 0)
            shifted = jnp.where(row == 0, prev_row, pltpu.roll(cur, 1, 0))
        return cur + mu_ref[...] * (shifted - cur)

    r = mix(r_ref, rp_ref, s0r_ref, mur_ref)
    k = mix(k_ref, kp_ref, s0k_ref, muk_ref)
    v = mix(v_ref, vp_ref, s0v_ref, muv_ref)
    t = mix(t_ref, tp_ref, s0t_ref, mut_ref)
    lw = t[:, :DECAY_LORA]
    la = t[:, DECAY_LORA:DECAY_LORA + AAA_LORA]
    lg = t[:, DECAY_LORA + AAA_LORA:]
    w_log = -_softplus(-(w0_ref[...] + _dot(jnp.tanh(lw), w2_ref[...], HIGHEST))) - 0.5
    a = jax.nn.sigmoid(a0_ref[...] + _dot(la, a2_ref[...], HIGHEST))
    kk = k * kk_ref[...]
    ssq = _dot(_dot(kk * kk, seg_ref[...], HIGHEST), segt_ref[...], HIGHEST)
    kkn = kk / jnp.maximum(jnp.sqrt(ssq), 1e-12)
    r_o[...] = r
    w_o[...] = jnp.exp(-jnp.exp(w_log))
    k_o[...] = k * (1.0 + (a - 1.0) * ka_ref[...])
    v_o[...] = v
    nkk_o[...] = -kkn
    kka_o[...] = kkn * a
    g_o[...] = _dot(jax.nn.sigmoid(lg), g2_ref[...])


def _rwkv_prep(main3, tail3, shift0, tmix_mu, w0, w2, a0, a2, g2, k_k, k_a, tm, single_step):
    b, t, _ = main3.shape
    rw = RWKV_WIDTH
    seg, segt = _head_segments(rw, H_RWKV)
    s0 = shift0.reshape(b, -1, RWKV_COLS)
    s0_main, s0_tail = s0[..., :RWKV_MAIN], s0[..., RWKV_MAIN:]
    mu = tmix_mu.reshape(1, RWKV_COLS)
    mu_main, mu_tail = mu[:, :RWKV_MAIN], mu[:, RWKV_MAIN:]
    prev = 8 if t >= 8 else t

    def cur(c, width=rw):
        return pl.BlockSpec((1, tm, width), lambda b_, i: (b_, i, c))

    def prv(c, width=rw):
        return pl.BlockSpec((1, prev, width), lambda b_, i: (b_, jnp.maximum(i * (tm // prev) - 1, 0), c))

    def s0s(c, width=rw):
        if single_step:
            return cur(c, width)
        return pl.BlockSpec((1, 1, width), lambda b_, i: (b_, 0, c))

    def par(c, rows=1, width=rw):
        return pl.BlockSpec((rows, width), lambda b_, i: (0, c))

    vec = lambda x: x.reshape(1, rw)
    out_spec = pl.BlockSpec((tm, rw), lambda b_, i: (i, b_))
    out_shape = jax.ShapeDtypeStruct((t, b * rw), F32)
    return pl.pallas_call(
        functools.partial(_rwkv_prep_body, single_step=single_step),
        grid=(b, t // tm),
        in_specs=[cur(0), cur(1), cur(2), cur(0, LORA_WIDTH),
                  prv(0), prv(1), prv(2), prv(0, LORA_WIDTH),
                  s0s(0), s0s(1), s0s(2), s0s(0, LORA_WIDTH),
                  par(0), par(1), par(2), par(0, 1, LORA_WIDTH),
                  par(0), par(0), par(0), par(0),
                  par(0, DECAY_LORA), par(0, AAA_LORA), par(0, GATE_LORA),
                  pl.BlockSpec((rw, LANES), lambda b_, i: (0, 0)),
                  pl.BlockSpec((LANES, rw), lambda b_, i: (0, 0))],
        out_specs=[out_spec] * 7,
        out_shape=[out_shape] * 7,
        compiler_params=_cparams(("arbitrary", "arbitrary"), 56),
        name="rwkv_prep",
    )(main3, main3, main3, tail3, main3, main3, main3, tail3,
      s0_main, s0_main, s0_main, s0_tail, mu_main, mu_main, mu_main, mu_tail,
      vec(w0), vec(a0), vec(k_k), vec(k_a), w2, a2, g2, seg, segt)


def _rwkv_scan_body(nkk_ref, w_ref, kka_ref, k_ref, r_ref, v_ref, s0_ref, y_ref, sf_ref, s_ref, kt_ref, yt_ref,
                    *, tt, n_heads):
    first_head = pl.program_id(0) * LANES
    tb = pl.program_id(1)

    @pl.when(tb == 0)
    def _():
        s_ref[...] = s0_ref[...]

    def to_lanes(x):
        xt = x.T
        if n_heads % LANES:
            lane = lax.broadcasted_iota(jnp.int32, xt.shape, 1)
            xt = jnp.where(lane < n_heads - first_head, xt, 0.0)
        return xt

    def one_step(t, slot):
        for j, ref in enumerate((nkk_ref, w_ref, kka_ref, k_ref, r_ref, v_ref)):
            kt_ref[slot, j] = to_lanes(ref[t])
        for vc in range(HEAD_DIM // 8):
            ys = []
            for u in range(8):
                vi = vc * 8 + u
                s_old = s_ref[vi]
                sa = jnp.sum(s_old * kt_ref[slot, 0], axis=0, keepdims=True)
                s_new = (s_old * kt_ref[slot, 1] + sa * kt_ref[slot, 2]
                         + kt_ref[slot, 5, vi:vi + 1, :] * kt_ref[slot, 3])
                s_ref[vi] = s_new
                ys.append(jnp.sum(s_new * kt_ref[slot, 4], axis=0, keepdims=True))
            yt_ref[slot, vc * 8:vc * 8 + 8, :] = jnp.concatenate(ys, axis=0)
        y_ref[t] = yt_ref[slot].T

    def steps(i, _):
        for slot in range(per_iter):
            one_step(i * per_iter + slot, slot)
        return 0

    per_iter = 2 if tt % 2 == 0 else 1
    lax.fori_loop(0, tt // per_iter, steps, 0)

    @pl.when(tb == pl.num_programs(1) - 1)
    def _():
        sf_ref[...] = s_ref[...]


def _rwkv_scan(nkk, w, kka, k, r, v, s0, tt):
    t, n_heads, e = r.shape
    groups = s0.shape[2] // LANES
    seq = pl.BlockSpec((tt, LANES, e), lambda g, i: (i, g, 0))
    st = pl.BlockSpec((e, e, LANES), lambda g, i: (0, 0, g))
    return pl.pallas_call(
        functools.partial(_rwkv_scan_body, tt=tt, n_heads=n_heads),
        grid=(groups, t // tt),
        in_specs=[seq] * 6 + [st],
        out_specs=[seq, st],
        out_shape=[jax.ShapeDtypeStruct((t, n_heads, e), F32), jax.ShapeDtypeStruct((e, e, groups * LANES), F32)],
        scratch_shapes=[pltpu.VMEM((e, e, LANES), F32), pltpu.VMEM((2, 6, e, LANES), F32),
                        pltpu.VMEM((2, e, LANES), F32)],
        compiler_params=_cparams(("arbitrary", "arbitrary"), 48),
        name="rwkv_scan",
    )(nkk, w, kka, k, r, v, s0)


def _rwkv_post_body(y_ref, r_ref, k_ref, v_ref, g_ref, lnw_ref, lnb_ref, rk_ref, seg_ref, segt_ref, o_ref):
    seg = seg_ref[...]
    segt = segt_ref[...]
    head_sum = lambda x: _dot(_dot(x, seg, HIGHEST), segt, HIGHEST)
    y = y_ref[...]
    yc = y - head_sum(y) * (1.0 / HEAD_DIM)
    var = head_sum(yc * yc) * (1.0 / HEAD_DIM)
    yn = yc * lax.rsqrt(var + GN_EPS) * lnw_ref[...] + lnb_ref[...]
    bonus = head_sum(r_ref[...] * k_ref[...] * rk_ref[...]) * v_ref[...]
    o_ref[0] = ((yn + bonus) * g_ref[...]).astype(o_ref.dtype)


def _rwkv_post(y, r, k, v, g, ln_w, ln_b, r_k, b, tm):
    t = y.shape[0]
    rw = RWKV_WIDTH
    seg, segt = _head_segments(rw, H_RWKV)
    blk = pl.BlockSpec((tm, rw), lambda b_, i: (i, b_))
    par = pl.BlockSpec((1, rw), lambda b_, i: (0, 0))
    return pl.pallas_call(
        _rwkv_post_body,
        grid=(b, t // tm),
        in_specs=[blk] * 5 + [par] * 3 + [pl.BlockSpec((rw, LANES), lambda b_, i: (0, 0)),
                                          pl.BlockSpec((LANES, rw), lambda b_, i: (0, 0))],
        out_specs=pl.BlockSpec((1, tm, rw), lambda b_, i: (b_, i, 0)),
        out_shape=jax.ShapeDtypeStruct((b, t, rw), BF16),
        compiler_params=_cparams(("arbitrary", "arbitrary"), 40),
        name="rwkv_post",
    )(y, r, k, v, g, ln_w.reshape(1, rw), ln_b.reshape(1, rw), r_k.reshape(1, rw), seg, segt)


def _rwkv_time_mix(main3, tail3, shift0, state0, tmix_mu, w0, w2, a0, a2, g2, k_k, k_a, r_k, ln_w, ln_b,
                   tm, tt, single_step):
    b, t, _ = main3.shape
    r, w, k, v, nkk, kka, g = _rwkv_prep(main3, tail3, shift0, tmix_mu, w0, w2, a0, a2, g2, k_k, k_a, tm, single_step)
    nseq, steps = (t, 1) if single_step else (b, t)
    n = nseq * H_RWKV
    lanes = -(-n // LANES) * LANES
    s0 = state0.astype(F32).reshape(n, HEAD_DIM, HEAD_DIM).transpose(1, 2, 0)
    s0 = jnp.pad(s0, ((0, 0), (0, 0), (0, lanes - n)))
    per_head = lambda x: x.reshape(steps, n, HEAD_DIM)
    ys, sf = _rwkv_scan(*(per_head(x) for x in (nkk, w, kka, k, r, v)), s0, tt)
    state = sf[:, :, :n].transpose(2, 0, 1).reshape(nseq, H_RWKV, HEAD_DIM, HEAD_DIM)
    return _rwkv_post(ys.reshape(t, b * RWKV_WIDTH), r, k, v, g, ln_w, ln_b, r_k, b, tm), state


PLAN_EXPERT, PLAN_FIRST, PLAN_SLOT, PLAN_NEXT = range(4)


def _run_plan(blk_expert):
    n = blk_expert.shape[0]
    first = jnp.concatenate([jnp.ones((1,), jnp.int32), (blk_expert[1:] != blk_expert[:-1]).astype(jnp.int32)])
    run = jnp.cumsum(first) - 1
    nxt_start = jnp.searchsorted(run, run + 1, side="left")
    nxt = jnp.where(nxt_start < n, blk_expert[jnp.minimum(nxt_start, n - 1)], -1)
    return jnp.stack([blk_expert, first, run % 2, nxt]).astype(jnp.int32)


def _prefetched_weights(plan_ref, m, make_copies):
    slot = plan_ref[PLAN_SLOT, m]

    @pl.when(plan_ref[PLAN_FIRST, m] == 1)
    def _():
        @pl.when(m == 0)
        def _():
            for c in make_copies(plan_ref[PLAN_EXPERT, m], slot):
                c.start()

        for c in make_copies(plan_ref[PLAN_EXPERT, m], slot):
            c.wait()

        @pl.when(plan_ref[PLAN_NEXT, m] >= 0)
        def _():
            for c in make_copies(plan_ref[PLAN_NEXT, m], 1 - slot):
                c.start()

    return slot


def _moe_up_body(plan_ref, x_ref, wg_hbm, wu_hbm, h_ref, wg_buf, wu_buf, sem):
    tf = h_ref.shape[1]
    cols = pl.ds(pl.multiple_of(pl.program_id(0) * tf, tf), tf)

    def copies(e, s):
        return (pltpu.make_async_copy(wg_hbm.at[e, :, cols], wg_buf.at[s], sem.at[0, s]),
                pltpu.make_async_copy(wu_hbm.at[e, :, cols], wu_buf.at[s], sem.at[1, s]))

    slot = _prefetched_weights(plan_ref, pl.program_id(1), copies)
    lo, hi = _unpack_bf16_pairs(_load_slabs(x_ref))
    x = jnp.concatenate([lo, hi], axis=1).astype(BF16)
    g = _dot(x, wg_buf[slot])
    h_ref[...] = (g * jax.nn.sigmoid(g) * _dot(x, wu_buf[slot])).astype(h_ref.dtype)


def _moe_down_body(plan_ref, h_ref, wd_hbm, y_ref, wd_buf, sem):
    def copies(e, s):
        return (pltpu.make_async_copy(wd_hbm.at[e], wd_buf.at[s], sem.at[s]),)

    slot = _prefetched_weights(plan_ref, pl.program_id(0), copies)
    _store_slabs(y_ref, _pack_bf16_pairs(_dot(h_ref[...], wd_buf[slot])))


def _grouped_mlp(xs, blk_expert, w_gate, w_up, w_down, tf=512):
    rows = xs.shape[0]
    nblk = rows // MOE_ROWS
    _, d, ff = w_gate.shape
    plan = _run_plan(blk_expert)
    any_spec = pl.BlockSpec(memory_space=pl.ANY)
    slab = lambda: pl.BlockSpec((MOE_ROWS, SLAB_ROWS, LANES), lambda *a: (a[-2], 0, 0))
    h = pl.pallas_call(
        _moe_up_body,
        grid_spec=pltpu.PrefetchScalarGridSpec(
            num_scalar_prefetch=1,
            grid=(ff // tf, nblk),
            in_specs=[slab(), any_spec, any_spec],
            out_specs=pl.BlockSpec((MOE_ROWS, tf), lambda f, m, plan_: (m, f)),
            scratch_shapes=[pltpu.VMEM((2, d, tf), w_gate.dtype), pltpu.VMEM((2, d, tf), w_up.dtype),
                            pltpu.SemaphoreType.DMA((2, 2))]),
        out_shape=jax.ShapeDtypeStruct((rows, ff), BF16),
        compiler_params=_cparams(("arbitrary", "arbitrary"), 48),
        name="moe_up",
    )(plan, xs, w_gate, w_up)
    return pl.pallas_call(
        _moe_down_body,
        grid_spec=pltpu.PrefetchScalarGridSpec(
            num_scalar_prefetch=1,
            grid=(nblk,),
            in_specs=[pl.BlockSpec((MOE_ROWS, ff), lambda m, plan_: (m, 0)), any_spec],
            out_specs=slab(),
            scratch_shapes=[pltpu.VMEM((2, ff, d), w_down.dtype), pltpu.SemaphoreType.DMA((2,))]),
        out_shape=jax.ShapeDtypeStruct((rows, SLAB_ROWS, LANES), jnp.uint32),
        compiler_params=_cparams(("arbitrary",), 48),
        name="moe_down",
    )(plan, h, w_down)


def _dispatch_plan(eidx, n_blocks):
    onehot = jnp.sum((eidx[:, :, None] == jnp.arange(N_EXPERTS, dtype=jnp.int32)).astype(jnp.int32), axis=1)
    before = jnp.cumsum(onehot, axis=0) - onehot
    rank = jnp.take_along_axis(before, eidx, axis=1)
    counts = jnp.sum(onehot, axis=0)
    padded = (counts + MOE_ROWS - 1) // MOE_ROWS * MOE_ROWS
    pad_end = jnp.cumsum(padded)
    dest = (pad_end - padded)[eidx] + rank
    blk_start = jnp.arange(n_blocks, dtype=jnp.int32) * MOE_ROWS
    blk_expert = jnp.minimum(jnp.searchsorted(pad_end, blk_start, side="right"), N_EXPERTS - 1).astype(jnp.int32)
    return dest.astype(jnp.int32), blk_expert


def _token_tiles(a, tm):
    n = a.shape[0]
    n_tiles = -(-n // tm)
    a = jnp.pad(a, ((0, n_tiles * tm - n), (0, 0)))
    return a.reshape(n_tiles, tm, TOP_K).transpose(0, 2, 1)


def _dispatch_body(dest_hbm, h_ref, xs_in_hbm, xs_hbm, dest_s, idx_sem, sem, *, tm, n_tok):
    del xs_in_hbm
    base = pl.program_id(0) * tm
    table = pltpu.make_async_copy(dest_hbm.at[pl.program_id(0)], dest_s, idx_sem)
    table.start()
    table.wait()

    def copies(r):
        return [pltpu.make_async_copy(h_ref.at[r], xs_hbm.at[dest_s[k, r]], sem) for k in range(TOP_K)]

    def issue(r, _):
        @pl.when(base + r < n_tok)
        def _():
            for c in copies(r):
                c.start()
        return 0

    def drain(r, _):
        @pl.when(base + r < n_tok)
        def _():
            for c in copies(r):
                c.wait()
        return 0

    lax.fori_loop(0, tm, issue, 0)
    lax.fori_loop(0, tm, drain, 0)


def _dispatch(h_slabs, dest, n_rows, tm=128):
    n_tok = dest.shape[0]
    tiles = _token_tiles(dest, tm)
    any_spec = pl.BlockSpec(memory_space=pl.ANY)
    return pl.pallas_call(
        functools.partial(_dispatch_body, tm=tm, n_tok=n_tok),
        grid=(tiles.shape[0],),
        in_specs=[any_spec, pl.BlockSpec((tm, SLAB_ROWS, LANES), lambda i: (i, 0, 0)), any_spec],
        out_specs=any_spec,
        out_shape=jax.ShapeDtypeStruct((n_rows, SLAB_ROWS, LANES), jnp.uint32),
        scratch_shapes=[pltpu.SMEM((TOP_K, tm), jnp.int32), pltpu.SemaphoreType.DMA(()), pltpu.SemaphoreType.DMA(())],
        input_output_aliases={2: 0},
        compiler_params=_cparams(("arbitrary",)),
        name="moe_dispatch",
    )(tiles, h_slabs, jnp.zeros((n_rows, SLAB_ROWS, LANES), jnp.uint32))


def _final_body(dest_hbm, gate_hbm, x_ref, shared_ref, gt_ref, fn_ref, yb_hbm, o_ref,
                dest_s, gate_s, gbuf, routed, idx_sem, sem, *, tm):
    i = pl.program_id(0)
    tables = [pltpu.make_async_copy(dest_hbm.at[i], dest_s, idx_sem.at[0]),
              pltpu.make_async_copy(gate_hbm.at[i], gate_s, idx_sem.at[1])]
    for c in tables:
        c.start()
    for c in tables:
        c.wait()

    def copies(r):
        return [pltpu.make_async_copy(yb_hbm.at[dest_s[k, r]], gbuf.at[r * TOP_K + k], sem) for k in range(TOP_K)]

    def issue(r, _):
        for c in copies(r):
            c.start()
        return 0

    def drain(r, _):
        for c in copies(r):
            c.wait()
        return 0

    lax.fori_loop(0, tm, issue, 0)
    lax.fori_loop(0, tm, drain, 0)

    def combine(r, _):
        lo = jnp.zeros((SLAB_ROWS, LANES), F32)
        hi = jnp.zeros((SLAB_ROWS, LANES), F32)
        for k in range(TOP_K):
            g = gate_s[k, r]
            l, h = _unpack_bf16_pairs(gbuf[r * TOP_K + k])
            lo = lo + g * l
            hi = hi + g * h
        routed[r, :SLAB_ROWS, :] = lo
        routed[r, SLAB_ROWS:, :] = hi
        return 0

    lax.fori_loop(0, tm, combine, 0)
    s_lo, s_hi = _unpack_bf16_pairs(shared_ref[...])
    x = x_ref[...] + gt_ref[...] * (routed[...] + jnp.concatenate([s_lo, s_hi], axis=1))
    ms = jnp.sum(jnp.sum(x * x, axis=2, keepdims=True), axis=1, keepdims=True) * (1.0 / D_MODEL)
    o_ref[...] = x * lax.rsqrt(ms + RMS_EPS) * fn_ref[...]


def _final(x_slabs, shared_slabs, shared_tile0, yb, dest, gate, mod_slabs, gt_tiles_per_row, final_norm, tm):
    n = x_slabs.shape[0]
    rows = 2 * SLAB_ROWS
    any_spec = pl.BlockSpec(memory_space=pl.ANY)
    tok = pl.BlockSpec((tm, rows, LANES), lambda i: (i, 0, 0))
    if gt_tiles_per_row:
        gt_spec = pl.BlockSpec((1, rows, LANES), lambda i: (i // gt_tiles_per_row, 5, 0))
    else:
        gt_spec = pl.BlockSpec((tm, rows, LANES), lambda i: (i, 5, 0))
    return pl.pallas_call(
        functools.partial(_final_body, tm=tm),
        grid=(n // tm,),
        in_specs=[any_spec, any_spec, tok,
                  pl.BlockSpec((tm, SLAB_ROWS, LANES), lambda i: (i + shared_tile0, 0, 0)),
                  gt_spec, pl.BlockSpec((1, rows, LANES), lambda i: (0, 0, 0)), any_spec],
        out_specs=tok,
        out_shape=jax.ShapeDtypeStruct((n, rows, LANES), F32),
        scratch_shapes=[pltpu.SMEM((TOP_K, tm), jnp.int32), pltpu.SMEM((TOP_K, tm), F32),
                        pltpu.VMEM((tm * TOP_K, SLAB_ROWS, LANES), jnp.uint32),
                        pltpu.VMEM((tm, rows, LANES), F32),
                        pltpu.SemaphoreType.DMA((2,)), pltpu.SemaphoreType.DMA(())],
        compiler_params=_cparams(("arbitrary",), 40),
        name="final",
    )(_token_tiles(dest, tm), _token_tiles(gate, tm), x_slabs, shared_slabs, mod_slabs,
      final_norm.reshape(1, rows, LANES), yb)


def _project(x3, mod3, norm1, w_in, tm_norm, tm):
    b, t, d = x3.shape
    h = _norm_mod(x3, norm1, mod3, 1, 0, tm_norm).reshape(b * t, d)
    tn = 512
    q, k, v = (_matmul(h, w_in, i * ATTN_WIDTH, ATTN_WIDTH, tm, tn) for i in range(3))
    main = _matmul(h, w_in, ATTN_COLS, RWKV_MAIN, tm, tn)
    tail = _matmul(h, w_in[:, ATTN_COLS + RWKV_MAIN:], 0, LORA_WIDTH, tm, LORA_WIDTH)
    return q, k, v, main, tail


def kernel(x_prompt, x_sample, cache_attn_k, cache_attn_v, state_rwkv, state_shift, c_prompt, c_sample,
           rel_bias, norm1, norm2, final_norm, w_ada, b_ada, w_in, w_out, tmix_mu, w0, w2, a0, a2, g2,
           k_k, k_a, r_k, ln_w, ln_b, router_w, router_b, ws_gate, ws_up, ws_down, w_gate, w_up, w_down):
    bp, sp, d = x_prompt.shape
    bs = x_sample.shape[0]
    assert w_in.shape[0] == 1
    l = 0

    c_all = jnp.concatenate([c_prompt, c_sample, jnp.zeros((4, d), F32)], axis=0)
    mod = _ada_mod(c_all, w_ada[l], b_ada[l])
    mod_p = mod[:bp].reshape(bp, 1, 6 * d)
    mod_s = mod[bp:bp + bs].reshape(1, bs, 6 * d)
    xs3 = x_sample.reshape(1, bs, d)
    tm_p, tm_s, tm_norm = 512, bs, 256

    rwkv_par = (tmix_mu[l], w0[l], w2[l], a0[l], a2[l], g2[l], k_k[l], k_a[l], r_k[l], ln_w[l], ln_b[l])

    qp, kp, vp, main_p, tail_p = _project(x_prompt, mod_p, norm1[l], w_in[l], tm_norm, tm_p)
    heads = lambda a: a.reshape(bp, sp, H_ATTN, HEAD_DIM).transpose(0, 2, 1, 3).astype(BF16)
    o_attn_p = _prompt_attention(heads(qp), heads(kp), heads(vp), rel_bias)
    o_attn_p = o_attn_p.transpose(0, 2, 1, 3).reshape(bp, sp, ATTN_WIDTH)
    main_p3 = main_p.reshape(bp, sp, RWKV_MAIN)
    tail_p3 = tail_p.reshape(bp, sp, LORA_WIDTH)
    o_rwkv_p, state_p = _rwkv_time_mix(main_p3, tail_p3, jnp.zeros((bp, RWKV_COLS), F32),
                                       jnp.zeros((bp, H_RWKV, HEAD_DIM, HEAD_DIM), F32), *rwkv_par,
                                       tm=128, tt=32, single_step=False)
    shift_p = jnp.concatenate([main_p3[:, -1], tail_p3[:, -1]], axis=-1)
    x1_p = _matmul_gated_residual(jnp.concatenate([o_attn_p, o_rwkv_p], axis=-1), w_out[l], x_prompt, mod_p, 2, tm_p)
    router = (router_w[l], router_b[l])
    h2_p, e_p, g_p = _norm_mod(x1_p, norm2[l], mod_p, 4, 3, tm_norm, router)

    qs, ks, vs, main_s, tail_s = _project(xs3, mod_s, norm1[l], w_in[l], tm_s, tm_s)
    hd = lambda a: a.reshape(bs, H_ATTN, HEAD_DIM)
    o_attn_s = _sample_attention(hd(qs), hd(ks), hd(vs), cache_attn_k[l], cache_attn_v[l], rel_bias).astype(BF16)
    o_rwkv_s, state_s = _rwkv_time_mix(main_s.reshape(1, bs, RWKV_MAIN), tail_s.reshape(1, bs, LORA_WIDTH),
                                       state_shift[l].reshape(1, bs, RWKV_COLS), state_rwkv[l], *rwkv_par,
                                       tm=bs, tt=1, single_step=True)
    shift_s = jnp.concatenate([main_s, tail_s], axis=-1)
    o_cat_s = jnp.concatenate([o_attn_s.reshape(1, bs, ATTN_WIDTH), o_rwkv_s], axis=-1)
    x1_s = _matmul_gated_residual(o_cat_s, w_out[l], xs3, mod_s, 2, tm_s)
    h2_s, e_s, g_s = _norm_mod(x1_s, norm2[l], mod_s, 4, 3, tm_s, router)

    n_p, n_s = bp * sp, bs
    n_all = n_p + n_s
    n_pad = -(-n_all // MOE_ROWS) * MOE_ROWS
    h_slabs = jnp.concatenate([h2_p, h2_s, jnp.zeros((n_pad - n_all, SLAB_ROWS, LANES), jnp.uint32)], axis=0)
    top = lambda a, b: jnp.concatenate([a.reshape(n_p, LANES)[:, :TOP_K], b.reshape(n_s, LANES)[:, :TOP_K]], axis=0)
    eidx, gate = top(e_p, e_s), top(g_p, g_s)
    n_blocks = -(-(n_all * TOP_K) // MOE_ROWS) + N_EXPERTS
    dest, blk_expert = _dispatch_plan(eidx, n_blocks)
    yb = _grouped_mlp(_dispatch(h_slabs, dest, n_blocks * MOE_ROWS), blk_expert, w_gate[l], w_up[l], w_down[l])
    shared = _grouped_mlp(h_slabs, jnp.zeros((n_pad // MOE_ROWS,), jnp.int32), ws_gate, ws_up, ws_down)

    rows = 2 * SLAB_ROWS
    tm_f = 128
    y_p = _final(x1_p.reshape(n_p, rows, LANES), shared, 0, yb, dest[:n_p], gate[:n_p],
                 mod_p.reshape(bp, 6 * rows, LANES), sp // tm_f, final_norm, tm_f)
    y_s = _final(x1_s.reshape(n_s, rows, LANES), shared, n_p // n_s, yb, dest[n_p:], gate[n_p:],
                 mod_s.reshape(bs, 6 * rows, LANES), 0, final_norm, n_s)
    y_p = y_p.reshape(bp, sp, d)

    return (y_p, y_s.reshape(bs, 1, d),
            kp.reshape(1, bp, sp, H_ATTN, HEAD_DIM), vp.reshape(1, bp, sp, H_ATTN, HEAD_DIM),
            state_p[None], shift_p[None],
            ks.reshape(1, bs, 1, H_ATTN, HEAD_DIM), vs.reshape(1, bs, 1, H_ATTN, HEAD_DIM),
            state_s[None], shift_s[None])
```

```python
import functools
import math

import numpy as np
import jax
import jax.numpy as jnp
from jax import lax
from jax.experimental import pallas as pl
from jax.experimental.pallas import tpu as pltpu

F32 = jnp.float32
BF16 = jnp.bfloat16
HIGHEST = lax.Precision.HIGHEST

D_MODEL = 4096
HEAD_DIM = 64
H_ATTN = 24
ATTN_WIDTH = H_ATTN * HEAD_DIM
H_RWKV = 40
RWKV_WIDTH = H_RWKV * HEAD_DIM
DILATIONS = ((128, 1), (512, 4), (2048, 16))
NUM_BUCKETS = 32
MAX_DISTANCE = 2048
DECAY_LORA = 128
AAA_LORA = 128
GATE_LORA = 480
LORA_WIDTH = DECAY_LORA + AAA_LORA + GATE_LORA
ATTN_COLS = 3 * ATTN_WIDTH
RWKV_MAIN = 3 * RWKV_WIDTH
RWKV_COLS = RWKV_MAIN + LORA_WIDTH
N_EXPERTS = 256
TOP_K = 8
N_GROUPS = 8
TOPK_GROUPS = 4
EXPERT_FF = 1024
ROUTED_SCALE = 2.5
RMS_EPS = 1e-6
GN_EPS = 64e-5
ATTN_SCALE = HEAD_DIM ** -0.5
NEG = -1e30

LANES = 128
SLAB_ROWS = D_MODEL // 2 // LANES
MOE_ROWS = 128
ATTN_TILE = 256
MIB = 1 << 20


def _cparams(dims, vmem_mib=None):
    kw = dict(dimension_semantics=dims)
    if vmem_mib is not None:
        kw["vmem_limit_bytes"] = vmem_mib * MIB
    return pltpu.CompilerParams(**kw)


def _dot(a, b, precision=None):
    return lax.dot_general(a, b, (((a.ndim - 1,), (0,)), ((), ())),
                           precision=precision, preferred_element_type=F32)


def _ada_body(c_ref, w_ref, b_ref, o_ref):
    c = c_ref[...]
    o_ref[...] = _dot(c * jax.nn.sigmoid(c), w_ref[...]) + b_ref[...]


def _ada_mod(c_all, w_ada, b_ada, tn=512):
    m, k = c_all.shape
    n = w_ada.shape[1]
    return pl.pallas_call(
        _ada_body,
        grid=(n // tn,),
        in_specs=[pl.BlockSpec((m, k), lambda j: (0, 0)),
                  pl.BlockSpec((k, tn), lambda j: (0, j)),
                  pl.BlockSpec((1, tn), lambda j: (0, j))],
        out_specs=pl.BlockSpec((m, tn), lambda j: (0, j)),
        out_shape=jax.ShapeDtypeStruct((m, n), F32),
        compiler_params=_cparams(("arbitrary",), 40),
        name="ada_mod",
    )(c_all, w_ada, b_ada.reshape(1, n))


def _norm_mod_body(x_ref, g_ref, sc_ref, sh_ref, o_ref):
    x = x_ref[0]
    y = x * lax.rsqrt(jnp.mean(x * x, axis=-1, keepdims=True) + RMS_EPS) * g_ref[...]
    o_ref[0] = (y * (1.0 + sc_ref[0]) + sh_ref[0]).astype(o_ref.dtype)


def _pack_bf16_pairs(h):
    bits = lax.bitcast_convert_type(h.astype(BF16).astype(F32), jnp.uint32)
    half = h.shape[1] // 2
    return (bits[:, half:] & jnp.uint32(0xFFFF0000)) | (bits[:, :half] >> 16)


def _unpack_bf16_pairs(w):
    return (lax.bitcast_convert_type(w << 16, F32),
            lax.bitcast_convert_type(w & jnp.uint32(0xFFFF0000), F32))


def _store_slabs(ref, words):
    for j in range(SLAB_ROWS):
        ref[:, j, :] = words[:, j * LANES:(j + 1) * LANES]


def _load_slabs(ref):
    return jnp.concatenate([ref[:, j, :] for j in range(SLAB_ROWS)], axis=1)


def _top_k_route(scores, router_b):
    m = scores.shape[0]
    gsz = N_EXPERTS // N_GROUPS
    lane = lax.broadcasted_iota(jnp.int32, scores.shape, 1).astype(F32)
    grp = jnp.floor(lane * (1.0 / gsz))
    choice = scores + router_b
    ninf = -jnp.inf
    row_max = lambda x: jnp.max(x, axis=-1, keepdims=True)
    row_min = lambda x: jnp.min(x, axis=-1, keepdims=True)
    gscore = jnp.zeros_like(choice)
    for g in range(N_GROUPS):
        ing = grp == float(g)
        cg = jnp.where(ing, choice, ninf)
        m1 = row_max(cg)
        i1 = row_min(jnp.where(cg == m1, lane, float(N_EXPERTS)))
        m2 = row_max(jnp.where(lane == i1, ninf, cg))
        gscore = jnp.where(ing, m1 + m2, gscore)
    cand = jnp.full_like(choice, ninf)
    for _ in range(TOPK_GROUPS):
        best = row_max(gscore)
        gsel = row_min(jnp.where(gscore == best, grp, float(N_GROUPS)))
        hit = grp == gsel
        cand = jnp.where(hit, choice, cand)
        gscore = jnp.where(hit, ninf, gscore)
    out_lane = lax.broadcasted_iota(jnp.int32, (m, LANES), 1)
    ids = jnp.zeros((m, LANES), F32)
    sel = jnp.zeros((m, LANES), F32)
    for k in range(TOP_K):
        best = row_max(cand)
        idx = row_min(jnp.where(cand == best, lane, float(N_EXPERTS)))
        hit = lane == idx
        ids = jnp.where(out_lane == k, idx, ids)
        sel = jnp.where(out_lane == k, jnp.sum(jnp.where(hit, scores, 0.0), axis=-1, keepdims=True), sel)
        cand = jnp.where(hit, ninf, cand)
    return ids, sel / jnp.sum(sel, axis=-1, keepdims=True) * ROUTED_SCALE


def _norm_mod_router_body(x_ref, g_ref, sc_ref, sh_ref, rw_ref, rb_ref, o_ref, e_ref, w_ref):
    x = x_ref[0]
    y = x * lax.rsqrt(jnp.mean(x * x, axis=-1, keepdims=True) + RMS_EPS) * g_ref[...]
    h = y * (1.0 + sc_ref[0]) + sh_ref[0]
    _store_slabs(o_ref, _pack_bf16_pairs(h))
    ids, gate = _top_k_route(jax.nn.sigmoid(_dot(h, rw_ref[...])), rb_ref[...])
    e_ref[0] = ids.astype(jnp.int32)
    w_ref[0] = gate


def _mod_spec(mod3, chunk, tm):
    if mod3.shape[1] == 1:
        return pl.BlockSpec((1, 1, D_MODEL), lambda b, i: (b, 0, chunk))
    return pl.BlockSpec((1, tm, D_MODEL), lambda b, i: (b, i, chunk))


def _norm_mod(x3, g, mod3, sc_chunk, sh_chunk, tm, router=None):
    b, t, d = x3.shape
    in_specs = [pl.BlockSpec((1, tm, d), lambda b_, i: (b_, i, 0)),
                pl.BlockSpec((1, d), lambda b_, i: (0, 0)),
                _mod_spec(mod3, sc_chunk, tm),
                _mod_spec(mod3, sh_chunk, tm)]
    h_spec = pl.BlockSpec((1, tm, d), lambda b_, i: (b_, i, 0))
    h_shape = jax.ShapeDtypeStruct((b, t, d), BF16)
    if router is None:
        return pl.pallas_call(
            _norm_mod_body, grid=(b, t // tm), in_specs=in_specs, out_specs=h_spec, out_shape=h_shape,
            compiler_params=_cparams(("arbitrary", "arbitrary"), 40), name="norm_mod",
        )(x3, g.reshape(1, d), mod3, mod3)
    router_w, router_b = router
    nt = t // tm
    in_specs += [pl.BlockSpec((d, N_EXPERTS), lambda b_, i: (0, 0)),
                 pl.BlockSpec((1, N_EXPERTS), lambda b_, i: (0, 0))]
    sel_spec = pl.BlockSpec((1, tm, LANES), lambda b_, i: (b_, i, 0))
    return pl.pallas_call(
        _norm_mod_router_body, grid=(b, nt), in_specs=in_specs,
        out_specs=[pl.BlockSpec((tm, SLAB_ROWS, LANES), lambda b_, i: (b_ * nt + i, 0, 0)), sel_spec, sel_spec],
        out_shape=[jax.ShapeDtypeStruct((b * t, SLAB_ROWS, LANES), jnp.uint32),
                   jax.ShapeDtypeStruct((b, t, LANES), jnp.int32),
                   jax.ShapeDtypeStruct((b, t, LANES), F32)],
        compiler_params=_cparams(("arbitrary", "arbitrary"), 48), name="norm_mod_router",
    )(x3, g.reshape(1, d), mod3, mod3, router_w, router_b.reshape(1, N_EXPERTS))


def _mm_body(x_ref, w_ref, o_ref):
    o_ref[...] = _dot(x_ref[...], w_ref[...]).astype(o_ref.dtype)


def _matmul(x, w, col0, n_cols, tm, tn, out_dtype=F32):
    m, k = x.shape
    assert col0 % tn == 0 and n_cols % tn == 0 and m % tm == 0
    off = col0 // tn
    return pl.pallas_call(
        _mm_body,
        grid=(n_cols // tn, m // tm),
        in_specs=[pl.BlockSpec((tm, k), lambda j, i: (i, 0)),
                  pl.BlockSpec((k, tn), lambda j, i: (0, j + off))],
        out_specs=pl.BlockSpec((tm, tn), lambda j, i: (i, j)),
        out_shape=jax.ShapeDtypeStruct((m, n_cols), out_dtype),
        compiler_params=_cparams(("arbitrary", "arbitrary"), 48),
        name="matmul",
    )(x, w)


def _mm_res_body(x_ref, w_ref, res_ref, g_ref, o_ref):
    o_ref[0] = res_ref[0] + g_ref[0] * _dot(x_ref[0], w_ref[...])


def _matmul_gated_residual(x3, w, res3, mod3, gate_chunk, tm, tn=512):
    b, t, k = x3.shape
    n = w.shape[1]
    nt = n // tn
    if mod3.shape[1] == 1:
        g_spec = pl.BlockSpec((1, 1, tn), lambda j, b_, i: (b_, 0, gate_chunk * nt + j))
    else:
        g_spec = pl.BlockSpec((1, tm, tn), lambda j, b_, i: (b_, i, gate_chunk * nt + j))
    return pl.pallas_call(
        _mm_res_body,
        grid=(nt, b, t // tm),
        in_specs=[pl.BlockSpec((1, tm, k), lambda j, b_, i: (b_, i, 0)),
                  pl.BlockSpec((k, tn), lambda j, b_, i: (0, j)),
                  pl.BlockSpec((1, tm, tn), lambda j, b_, i: (b_, i, j)),
                  g_spec],
        out_specs=pl.BlockSpec((1, tm, tn), lambda j, b_, i: (b_, i, j)),
        out_shape=jax.ShapeDtypeStruct((b, t, n), F32),
        compiler_params=_cparams(("arbitrary", "arbitrary", "arbitrary"), 48),
        name="matmul_gated_residual",
    )(x3, w, res3, mod3)


def _rel_bucket(dist):
    n_exact = NUM_BUCKETS // 2
    df = jnp.maximum(dist, 1).astype(F32)
    large = n_exact + (jnp.log(df / n_exact) / math.log(MAX_DISTANCE / n_exact)
                       * (NUM_BUCKETS - n_exact)).astype(jnp.int32)
    return jnp.where(dist < n_exact, dist, jnp.minimum(large, NUM_BUCKETS - 1))


def _distance_bias(rel_bias, seq):
    dist = np.arange(seq)
    mult = np.zeros(seq, np.float64)
    for window, dil in DILATIONS:
        mult += (dist % dil == 0) & (dist <= window)
    logm = jnp.asarray(np.log(np.maximum(mult, 1.0)), F32)
    bias = rel_bias.astype(F32)[_rel_bucket(jnp.asarray(dist, jnp.int32))] + logm[:, None]
    bias = jnp.where(jnp.asarray(mult > 0)[:, None], bias, NEG)
    return bias.T


def _toeplitz_body(p_ref, o_ref):
    tile = o_ref.shape[-1]
    x = jnp.broadcast_to(p_ref[0, 0], (tile, 2 * tile))
    o_ref[0, 0] = pltpu.roll(x, 0, 1, stride=1, stride_axis=0)[:, :tile]


def _toeplitz_bias(rel_bias, seq, tile):
    nd = seq // tile
    cb = _distance_bias(rel_bias, seq)
    cb_ext = jnp.concatenate([jnp.full((H_ATTN, tile), NEG, F32), cb], axis=1)
    m = np.arange(2 * tile)
    back = np.where(m < tile, -m, 2 * tile - m)
    idx = np.minimum(np.arange(nd)[:, None] * tile + back[None, :] + tile, seq + tile - 1).astype(np.int32)
    rows = jnp.take(cb_ext, jnp.asarray(idx.reshape(-1)), axis=1).reshape(H_ATTN, nd, 1, 2 * tile)
    return pl.pallas_call(
        _toeplitz_body,
        grid=(H_ATTN, nd),
        in_specs=[pl.BlockSpec((1, 1, 1, 2 * tile), lambda h, d: (h, d, 0, 0))],
        out_specs=pl.BlockSpec((1, 1, tile, tile), lambda h, d: (h, d, 0, 0)),
        out_shape=jax.ShapeDtypeStruct((H_ATTN, nd, tile, tile), F32),
        compiler_params=_cparams(("arbitrary", "arbitrary")),
        name="toeplitz_bias",
    )(rows)


def _attn_body(q_ref, k_ref, v_ref, b_ref, o_ref):
    qi = pl.program_id(2)
    q = (q_ref[0, 0].astype(F32) * ATTN_SCALE).astype(BF16)

    def body(j, carry):
        m, l, acc = carry
        rows = pl.ds(pl.multiple_of(j * ATTN_TILE, ATTN_TILE), ATTN_TILE)
        s = lax.dot_general(q, k_ref[0, 0, rows, :], (((1,), (1,)), ((), ())),
                            preferred_element_type=F32) + b_ref[0, qi - j]
        m_new = jnp.maximum(m, jnp.max(s, axis=-1, keepdims=True))
        alpha = jnp.exp(m - m_new)
        p = jnp.exp(s - m_new)
        l = alpha * l + jnp.sum(p, axis=-1, keepdims=True)
        acc = alpha * acc + _dot(p.astype(BF16), v_ref[0, 0, rows, :])
        return m_new, l, acc

    init = (jnp.full((ATTN_TILE, 1), NEG, F32), jnp.zeros((ATTN_TILE, 1), F32),
            jnp.zeros((ATTN_TILE, HEAD_DIM), F32))
    _, l, acc = lax.fori_loop(0, qi + 1, body, init)
    o_ref[0, 0] = (acc / l).astype(o_ref.dtype)


def _prompt_attention(q, k, v, rel_bias):
    b, h, s, e = q.shape
    tb = _toeplitz_bias(rel_bias, s, ATTN_TILE)
    return pl.pallas_call(
        _attn_body,
        grid=(b, h, s // ATTN_TILE),
        in_specs=[pl.BlockSpec((1, 1, ATTN_TILE, e), lambda b_, h_, i: (b_, h_, i, 0)),
                  pl.BlockSpec((1, 1, s, e), lambda b_, h_, i: (b_, h_, 0, 0)),
                  pl.BlockSpec((1, 1, s, e), lambda b_, h_, i: (b_, h_, 0, 0)),
                  pl.BlockSpec((1, s // ATTN_TILE, ATTN_TILE, ATTN_TILE), lambda b_, h_, i: (h_, 0, 0, 0))],
        out_specs=pl.BlockSpec((1, 1, ATTN_TILE, e), lambda b_, h_, i: (b_, h_, i, 0)),
        out_shape=jax.ShapeDtypeStruct((b, h, s, e), BF16),
        compiler_params=_cparams(("arbitrary", "arbitrary", "arbitrary"), 32),
        name="prompt_attention",
    )(q, k, v, tb)


def _head_segments(width, n_heads):
    seg = np.zeros((width, LANES), np.float32)
    seg[np.arange(width), np.arange(width) // HEAD_DIM] = 1.0
    assert n_heads <= LANES
    return jnp.asarray(seg), jnp.asarray(seg.T.copy())


def _sample_attn_body(q_ref, kn_ref, vn_ref, k_ref, v_ref, bias_ref, b0_ref, o_ref):
    q = q_ref[0] * ATTN_SCALE
    s0 = jnp.sum(kn_ref[0] * q, axis=1, keepdims=True) + b0_ref[...]
    s = jnp.sum(k_ref[0] * q, axis=1, keepdims=True) + bias_ref[...]
    m = jnp.maximum(s0, jnp.max(s, axis=2, keepdims=True))
    p0 = jnp.exp(s0 - m)
    p = jnp.exp(s - m)
    den = p0 + jnp.sum(p, axis=2, keepdims=True)
    num = p0 * vn_ref[0] + jnp.sum(v_ref[0] * p, axis=2, keepdims=True)
    o_ref[0] = num / den


def _sample_attention(q, k_new, v_new, k_buf, v_buf, rel_bias, heads_per_step=8):
    bd, h, e = q.shape
    wbuf = k_buf.shape[3]
    hs = heads_per_step
    cb = _distance_bias(rel_bias, wbuf + 1)
    bias = cb[:, :0:-1].reshape(h, 1, wbuf)
    b0 = cb[:, 0].reshape(h, 1, 1)
    col = lambda a: a.reshape(bd, h, e, 1)
    vec = pl.BlockSpec((1, hs, e, 1), lambda b_, j: (b_, j, 0, 0))
    buf = pl.BlockSpec((1, hs, e, wbuf), lambda b_, j: (b_, j, 0, 0))
    out = pl.pallas_call(
        _sample_attn_body,
        grid=(bd, h // hs),
        in_specs=[vec, vec, vec, buf, buf,
                  pl.BlockSpec((hs, 1, wbuf), lambda b_, j: (j, 0, 0)),
                  pl.BlockSpec((hs, 1, 1), lambda b_, j: (j, 0, 0))],
        out_specs=vec,
        out_shape=jax.ShapeDtypeStruct((bd, h, e, 1), F32),
        compiler_params=_cparams(("arbitrary", "arbitrary"), 48),
        name="sample_attention",
    )(col(q), col(k_new), col(v_new), k_buf, v_buf, bias, b0)
    return out.reshape(bd, h, e)


def _softplus(z):
    return jnp.maximum(z, 0.0) + jnp.log1p(jnp.exp(-jnp.abs(z)))


def _rwkv_prep_body(r_ref, k_ref, v_ref, t_ref, rp_ref, kp_ref, vp_ref, tp_ref,
                    s0r_ref, s0k_ref, s0v_ref, s0t_ref, mur_ref, muk_ref, muv_ref, mut_ref,
                    w0_ref, a0_ref, kk_ref, ka_ref, w2_ref, a2_ref, g2_ref, seg_ref, segt_ref,
                    r_o, w_o, k_o, v_o, nkk_o, kka_o, g_o, *, single_step):
    first = pl.program_id(1) == 0

    def mix(cur_ref, prev_ref, s0_ref, mu_ref):
        cur = cur_ref[0]
        if single_step:
            shifted = s0_ref[0]
        else:
            prev_row = jnp.where(first, s0_ref[0], prev_ref[0, 7:8, :])
            row = lax.broadcasted_iota(jnp.int32, cur.shape, 0)
            shifted = jnp.where(row == 0, prev_row, pltpu.roll(cur, 1, 0))
        return cur + mu_ref[...] * (shifted - cur)

    r = mix(r_ref, rp_ref, s0r_ref, mur_ref)
    k = mix(k_ref, kp_ref, s0k_ref, muk_ref)
    v = mix(v_ref, vp_ref, s0v_ref, muv_ref)
    t = mix(t_ref, tp_ref, s0t_ref, mut_ref)
    lw = t[:, :DECAY_LORA]
    la = t[:, DECAY_LORA:DECAY_LORA + AAA_LORA]
    lg = t[:, DECAY_LORA + AAA_LORA:]
    w_log = -_softplus(-(w0_ref[...] + _dot(jnp.tanh(lw), w2_ref[...], HIGHEST))) - 0.5
    a = jax.nn.sigmoid(a0_ref[...] + _dot(la, a2_ref[...], HIGHEST))
    kk = k * kk_ref[...]
    ssq = _dot(_dot(kk * kk, seg_ref[...], HIGHEST), segt_ref[...], HIGHEST)
    kkn = kk / jnp.maximum(jnp.sqrt(ssq), 1e-12)
    r_o[...] = r
    w_o[...] = jnp.exp(-jnp.exp(w_log))
    k_o[...] = k * (1.0 + (a - 1.0) * ka_ref[...])
    v_o[...] = v
    nkk_o[...] = -kkn
    kka_o[...] = kkn * a
    g_o[...] = _dot(jax.nn.sigmoid(lg), g2_ref[...])


def _rwkv_prep(main3, tail3, shift0, tmix_mu, w0, w2, a0, a2, g2, k_k, k_a, tm, single_step):
    b, t, _ = main3.shape
    rw = RWKV_WIDTH
    seg, segt = _head_segments(rw, H_RWKV)
    s0 = shift0.reshape(b, -1, RWKV_COLS)
    s0_main, s0_tail = s0[..., :RWKV_MAIN], s0[..., RWKV_MAIN:]
    mu = tmix_mu.reshape(1, RWKV_COLS)
    mu_main, mu_tail = mu[:, :RWKV_MAIN], mu[:, RWKV_MAIN:]
    prev = 8 if t >= 8 else t

    def cur(c, width=rw):
        return pl.BlockSpec((1, tm, width), lambda b_, i: (b_, i, c))

    def prv(c, width=rw):
        return pl.BlockSpec((1, prev, width), lambda b_, i: (b_, jnp.maximum(i * (tm // prev) - 1, 0), c))

    def s0s(c, width=rw):
        if single_step:
            return cur(c, width)
        return pl.BlockSpec((1, 1, width), lambda b_, i: (b_, 0, c))

    def par(c, rows=1, width=rw):
        return pl.BlockSpec((rows, width), lambda b_, i: (0, c))

    vec = lambda x: x.reshape(1, rw)
    out_spec = pl.BlockSpec((tm, rw), lambda b_, i: (i, b_))
    out_shape = jax.ShapeDtypeStruct((t, b * rw), F32)
    return pl.pallas_call(
        functools.partial(_rwkv_prep_body, single_step=single_step),
        grid=(b, t // tm),
        in_specs=[cur(0), cur(1), cur(2), cur(0, LORA_WIDTH),
                  prv(0), prv(1), prv(2), prv(0, LORA_WIDTH),
                  s0s(0), s0s(1), s0s(2), s0s(0, LORA_WIDTH),
                  par(0), par(1), par(2), par(0, 1, LORA_WIDTH),
                  par(0), par(0), par(0), par(0),
                  par(0, DECAY_LORA), par(0, AAA_LORA), par(0, GATE_LORA),
                  pl.BlockSpec((rw, LANES), lambda b_, i: (0, 0)),
                  pl.BlockSpec((LANES, rw), lambda b_, i: (0, 0))],
        out_specs=[out_spec] * 7,
        out_shape=[out_shape] * 7,
        compiler_params=_cparams(("arbitrary", "arbitrary"), 56),
        name="rwkv_prep",
    )(main3, main3, main3, tail3, main3, main3, main3, tail3,
      s0_main, s0_main, s0_main, s0_tail, mu_main, mu_main, mu_main, mu_tail,
      vec(w0), vec(a0), vec(k_k), vec(k_a), w2, a2, g2, seg, segt)


def _rwkv_scan_body(nkk_ref, w_ref, kka_ref, k_ref, r_ref, v_ref, s0_ref, y_ref, sf_ref, s_ref, kt_ref, yt_ref,
                    *, tt, n_heads):
    first_head = pl.program_id(0) * LANES
    tb = pl.program_id(1)

    @pl.when(tb == 0)
    def _():
        s_ref[...] = s0_ref[...]

    def to_lanes(x):
        xt = x.T
        if n_heads % LANES:
            lane = lax.broadcasted_iota(jnp.int32, xt.shape, 1)
            xt = jnp.where(lane < n_heads - first_head, xt, 0.0)
        return xt

    def one_step(t, slot):
        for j, ref in enumerate((nkk_ref, w_ref, kka_ref, k_ref, r_ref, v_ref)):
            kt_ref[slot, j] = to_lanes(ref[t])
        for vc in range(HEAD_DIM // 8):
            ys = []
            for u in range(8):
                vi = vc * 8 + u
                s_old = s_ref[vi]
                sa = jnp.sum(s_old * kt_ref[slot, 0], axis=0, keepdims=True)
                s_new = (s_old * kt_ref[slot, 1] + sa * kt_ref[slot, 2]
                         + kt_ref[slot, 5, vi:vi + 1, :] * kt_ref[slot, 3])
                s_ref[vi] = s_new
                ys.append(jnp.sum(s_new * kt_ref[slot, 4], axis=0, keepdims=True))
            yt_ref[slot, vc * 8:vc * 8 + 8, :] = jnp.concatenate(ys, axis=0)
        y_ref[t] = yt_ref[slot].T

    def steps(i, _):
        for slot in range(per_iter):
            one_step(i * per_iter + slot, slot)
        return 0

    per_iter = 2 if tt % 2 == 0 else 1
    lax.fori_loop(0, tt // per_iter, steps, 0)

    @pl.when(tb == pl.num_programs(1) - 1)
    def _():
        sf_ref[...] = s_ref[...]


def _rwkv_scan(nkk, w, kka, k, r, v, s0, tt):
    t, n_heads, e = r.shape
    groups = s0.shape[2] // LANES
    seq = pl.BlockSpec((tt, LANES, e), lambda g, i: (i, g, 0))
    st = pl.BlockSpec((e, e, LANES), lambda g, i: (0, 0, g))
    return pl.pallas_call(
        functools.partial(_rwkv_scan_body, tt=tt, n_heads=n_heads),
        grid=(groups, t // tt),
        in_specs=[seq] * 6 + [st],
        out_specs=[seq, st],
        out_shape=[jax.ShapeDtypeStruct((t, n_heads, e), F32), jax.ShapeDtypeStruct((e, e, groups * LANES), F32)],
        scratch_shapes=[pltpu.VMEM((e, e, LANES), F32), pltpu.VMEM((2, 6, e, LANES), F32),
                        pltpu.VMEM((2, e, LANES), F32)],
        compiler_params=_cparams(("arbitrary", "arbitrary"), 48),
        name="rwkv_scan",
    )(nkk, w, kka, k, r, v, s0)


def _rwkv_post_body(y_ref, r_ref, k_ref, v_ref, g_ref, lnw_ref, lnb_ref, rk_ref, seg_ref, segt_ref, o_ref):
    seg = seg_ref[...]
    segt = segt_ref[...]
    head_sum = lambda x: _dot(_dot(x, seg, HIGHEST), segt, HIGHEST)
    y = y_ref[...]
    yc = y - head_sum(y) * (1.0 / HEAD_DIM)
    var = head_sum(yc * yc) * (1.0 / HEAD_DIM)
    yn = yc * lax.rsqrt(var + GN_EPS) * lnw_ref[...] + lnb_ref[...]
    bonus = head_sum(r_ref[...] * k_ref[...] * rk_ref[...]) * v_ref[...]
    o_ref[0] = ((yn + bonus) * g_ref[...]).astype(o_ref.dtype)


def _rwkv_post(y, r, k, v, g, ln_w, ln_b, r_k, b, tm):
    t = y.shape[0]
    rw = RWKV_WIDTH
    seg, segt = _head_segments(rw, H_RWKV)
    blk = pl.BlockSpec((tm, rw), lambda b_, i: (i, b_))
    par = pl.BlockSpec((1, rw), lambda b_, i: (0, 0))
    return pl.pallas_call(
        _rwkv_post_body,
        grid=(b, t // tm),
        in_specs=[blk] * 5 + [par] * 3 + [pl.BlockSpec((rw, LANES), lambda b_, i: (0, 0)),
                                          pl.BlockSpec((LANES, rw), lambda b_, i: (0, 0))],
        out_specs=pl.BlockSpec((1, tm, rw), lambda b_, i: (b_, i, 0)),
        out_shape=jax.ShapeDtypeStruct((b, t, rw), BF16),
        compiler_params=_cparams(("arbitrary", "arbitrary"), 40),
        name="rwkv_post",
    )(y, r, k, v, g, ln_w.reshape(1, rw), ln_b.reshape(1, rw), r_k.reshape(1, rw), seg, segt)


def _rwkv_time_mix(main3, tail3, shift0, state0, tmix_mu, w0, w2, a0, a2, g2, k_k, k_a, r_k, ln_w, ln_b,
                   tm, tt, single_step):
    b, t, _ = main3.shape
    r, w, k, v, nkk, kka, g = _rwkv_prep(main3, tail3, shift0, tmix_mu, w0, w2, a0, a2, g2, k_k, k_a, tm, single_step)
    nseq, steps = (t, 1) if single_step else (b, t)
    n = nseq * H_RWKV
    lanes = -(-n // LANES) * LANES
    s0 = state0.astype(F32).reshape(n, HEAD_DIM, HEAD_DIM).transpose(1, 2, 0)
    s0 = jnp.pad(s0, ((0, 0), (0, 0), (0, lanes - n)))
    per_head = lambda x: x.reshape(steps, n, HEAD_DIM)
    ys, sf = _rwkv_scan(*(per_head(x) for x in (nkk, w, kka, k, r, v)), s0, tt)
    state = sf[:, :, :n].transpose(2, 0, 1).reshape(nseq, H_RWKV, HEAD_DIM, HEAD_DIM)
    return _rwkv_post(ys.reshape(t, b * RWKV_WIDTH), r, k, v, g, ln_w, ln_b, r_k, b, tm), state


PLAN_EXPERT, PLAN_FIRST, PLAN_SLOT, PLAN_NEXT = range(4)


def _run_plan(blk_expert):
    n = blk_expert.shape[0]
    first = jnp.concatenate([jnp.ones((1,), jnp.int32), (blk_expert[1:] != blk_expert[:-1]).astype(jnp.int32)])
    run = jnp.cumsum(first) - 1
    nxt_start = jnp.searchsorted(run, run + 1, side="left")
    nxt = jnp.where(nxt_start < n, blk_expert[jnp.minimum(nxt_start, n - 1)], -1)
    return jnp.stack([blk_expert, first, run % 2, nxt]).astype(jnp.int32)


def _prefetched_weights(plan_ref, m, make_copies):
    slot = plan_ref[PLAN_SLOT, m]

    @pl.when(plan_ref[PLAN_FIRST, m] == 1)
    def _():
        @pl.when(m == 0)
        def _():
            for c in make_copies(plan_ref[PLAN_EXPERT, m], slot):
                c.start()

        for c in make_copies(plan_ref[PLAN_EXPERT, m], slot):
            c.wait()

        @pl.when(plan_ref[PLAN_NEXT, m] >= 0)
        def _():
            for c in make_copies(plan_ref[PLAN_NEXT, m], 1 - slot):
                c.start()

    return slot


def _moe_up_body(plan_ref, x_ref, wg_hbm, wu_hbm, h_ref, wg_buf, wu_buf, sem):
    tf = h_ref.shape[1]
    cols = pl.ds(pl.multiple_of(pl.program_id(0) * tf, tf), tf)

    def copies(e, s):
        return (pltpu.make_async_copy(wg_hbm.at[e, :, cols], wg_buf.at[s], sem.at[0, s]),
                pltpu.make_async_copy(wu_hbm.at[e, :, cols], wu_buf.at[s], sem.at[1, s]))

    slot = _prefetched_weights(plan_ref, pl.program_id(1), copies)
    lo, hi = _unpack_bf16_pairs(_load_slabs(x_ref))
    x = jnp.concatenate([lo, hi], axis=1).astype(BF16)
    g = _dot(x, wg_buf[slot])
    h_ref[...] = (g * jax.nn.sigmoid(g) * _dot(x, wu_buf[slot])).astype(h_ref.dtype)


def _moe_down_body(plan_ref, h_ref, wd_hbm, y_ref, wd_buf, sem):
    def copies(e, s):
        return (pltpu.make_async_copy(wd_hbm.at[e], wd_buf.at[s], sem.at[s]),)

    slot = _prefetched_weights(plan_ref, pl.program_id(0), copies)
    _store_slabs(y_ref, _pack_bf16_pairs(_dot(h_ref[...], wd_buf[slot])))


def _grouped_mlp(xs, blk_expert, w_gate, w_up, w_down, tf=512):
    rows = xs.shape[0]
    nblk = rows // MOE_ROWS
    _, d, ff = w_gate.shape
    plan = _run_plan(blk_expert)
    any_spec = pl.BlockSpec(memory_space=pl.ANY)
    slab = lambda: pl.BlockSpec((MOE_ROWS, SLAB_ROWS, LANES), lambda *a: (a[-2], 0, 0))
    h = pl.pallas_call(
        _moe_up_body,
        grid_spec=pltpu.PrefetchScalarGridSpec(
            num_scalar_prefetch=1,
            grid=(ff // tf, nblk),
            in_specs=[slab(), any_spec, any_spec],
            out_specs=pl.BlockSpec((MOE_ROWS, tf), lambda f, m, plan_: (m, f)),
            scratch_shapes=[pltpu.VMEM((2, d, tf), w_gate.dtype), pltpu.VMEM((2, d, tf), w_up.dtype),
                            pltpu.SemaphoreType.DMA((2, 2))]),
        out_shape=jax.ShapeDtypeStruct((rows, ff), BF16),
        compiler_params=_cparams(("arbitrary", "arbitrary"), 48),
        name="moe_up",
    )(plan, xs, w_gate, w_up)
    return pl.pallas_call(
        _moe_down_body,
        grid_spec=pltpu.PrefetchScalarGridSpec(
            num_scalar_prefetch=1,
            grid=(nblk,),
            in_specs=[pl.BlockSpec((MOE_ROWS, ff), lambda m, plan_: (m, 0)), any_spec],
            out_specs=slab(),
            scratch_shapes=[pltpu.VMEM((2, ff, d), w_down.dtype), pltpu.SemaphoreType.DMA((2,))]),
        out_shape=jax.ShapeDtypeStruct((rows, SLAB_ROWS, LANES), jnp.uint32),
        compiler_params=_cparams(("arbitrary",), 48),
        name="moe_down",
    )(plan, h, w_down)


def _dispatch_plan(eidx, n_blocks):
    hit = eidx[:, :, None] == jnp.arange(N_EXPERTS, dtype=jnp.int32)
    onehot = jnp.sum(hit.astype(jnp.int32), axis=1)
    before = jnp.cumsum(onehot, axis=0) - onehot
    counts = jnp.sum(onehot, axis=0)
    padded = (counts + MOE_ROWS - 1) // MOE_ROWS * MOE_ROWS
    pad_end = jnp.cumsum(padded)
    dest = jnp.sum(jnp.where(hit, (before + (pad_end - padded))[:, None, :], 0), axis=-1)
    blk_start = jnp.arange(n_blocks, dtype=jnp.int32) * MOE_ROWS
    blk_expert = jnp.minimum(jnp.searchsorted(pad_end, blk_start, side="right"), N_EXPERTS - 1).astype(jnp.int32)
    return dest.astype(jnp.int32), blk_expert


def _token_tiles(a, tm):
    n = a.shape[0]
    n_tiles = -(-n // tm)
    a = jnp.pad(a, ((0, n_tiles * tm - n), (0, 0)))
    return a.reshape(n_tiles, tm, TOP_K).transpose(0, 2, 1)


def _dispatch_body(dest_hbm, h_ref, xs_in_hbm, xs_hbm, dest_s, idx_sem, sem, *, tm, n_tok):
    del xs_in_hbm
    base = pl.program_id(0) * tm
    table = pltpu.make_async_copy(dest_hbm.at[pl.program_id(0)], dest_s, idx_sem)
    table.start()
    table.wait()

    def copies(r):
        return [pltpu.make_async_copy(h_ref.at[r], xs_hbm.at[dest_s[k, r]], sem) for k in range(TOP_K)]

    def issue(r, _):
        @pl.when(base + r < n_tok)
        def _():
            for c in copies(r):
                c.start()
        return 0

    def drain(r, _):
        @pl.when(base + r < n_tok)
        def _():
            for c in copies(r):
                c.wait()
        return 0

    lax.fori_loop(0, tm, issue, 0)
    lax.fori_loop(0, tm, drain, 0)


def _dispatch(h_slabs, dest, n_rows, tm=128):
    n_tok = dest.shape[0]
    tiles = _token_tiles(dest, tm)
    any_spec = pl.BlockSpec(memory_space=pl.ANY)
    return pl.pallas_call(
        functools.partial(_dispatch_body, tm=tm, n_tok=n_tok),
        grid=(tiles.shape[0],),
        in_specs=[any_spec, pl.BlockSpec((tm, SLAB_ROWS, LANES), lambda i: (i, 0, 0)), any_spec],
        out_specs=any_spec,
        out_shape=jax.ShapeDtypeStruct((n_rows, SLAB_ROWS, LANES), jnp.uint32),
        scratch_shapes=[pltpu.SMEM((TOP_K, tm), jnp.int32), pltpu.SemaphoreType.DMA(()), pltpu.SemaphoreType.DMA(())],
        input_output_aliases={2: 0},
        compiler_params=_cparams(("arbitrary",)),
        name="moe_dispatch",
    )(tiles, h_slabs, jnp.zeros((n_rows, SLAB_ROWS, LANES), jnp.uint32))


def _final_body(dest_hbm, gate_hbm, x_ref, shared_ref, gt_ref, fn_ref, yb_hbm, o_ref,
                dest_s, gate_s, gbuf, routed, idx_sem, sem, *, tm):
    i = pl.program_id(0)
    tables = [pltpu.make_async_copy(dest_hbm.at[i], dest_s, idx_sem.at[0]),
              pltpu.make_async_copy(gate_hbm.at[i], gate_s, idx_sem.at[1])]
    for c in tables:
        c.start()
    for c in tables:
        c.wait()

    def copies(r):
        return [pltpu.make_async_copy(yb_hbm.at[dest_s[k, r]], gbuf.at[r * TOP_K + k], sem) for k in range(TOP_K)]

    def issue(r, _):
        for c in copies(r):
            c.start()
        return 0

    def drain(r, _):
        for c in copies(r):
            c.wait()
        return 0

    lax.fori_loop(0, tm, issue, 0)
    lax.fori_loop(0, tm, drain, 0)

    def combine(r, _):
        lo = jnp.zeros((SLAB_ROWS, LANES), F32)
        hi = jnp.zeros((SLAB_ROWS, LANES), F32)
        for k in range(TOP_K):
            g = gate_s[k, r]
            l, h = _unpack_bf16_pairs(gbuf[r * TOP_K + k])
            lo = lo + g * l
            hi = hi + g * h
        routed[r, :SLAB_ROWS, :] = lo
        routed[r, SLAB_ROWS:, :] = hi
        return 0

    lax.fori_loop(0, tm, combine, 0)
    s_lo, s_hi = _unpack_bf16_pairs(shared_ref[...])
    x = x_ref[...] + gt_ref[...] * (routed[...] + jnp.concatenate([s_lo, s_hi], axis=1))
    ms = jnp.sum(jnp.sum(x * x, axis=2, keepdims=True), axis=1, keepdims=True) * (1.0 / D_MODEL)
    o_ref[...] = x * lax.rsqrt(ms + RMS_EPS) * fn_ref[...]


def _final(x_slabs, shared_slabs, shared_tile0, yb, dest, gate, mod_slabs, gt_tiles_per_row, final_norm, tm):
    n = x_slabs.shape[0]
    rows = 2 * SLAB_ROWS
    any_spec = pl.BlockSpec(memory_space=pl.ANY)
    tok = pl.BlockSpec((tm, rows, LANES), lambda i: (i, 0, 0))
    if gt_tiles_per_row:
        gt_spec = pl.BlockSpec((1, rows, LANES), lambda i: (i // gt_tiles_per_row, 5, 0))
    else:
        gt_spec = pl.BlockSpec((tm, rows, LANES), lambda i: (i, 5, 0))
    return pl.pallas_call(
        functools.partial(_final_body, tm=tm),
        grid=(n // tm,),
        in_specs=[any_spec, any_spec, tok,
                  pl.BlockSpec((tm, SLAB_ROWS, LANES), lambda i: (i + shared_tile0, 0, 0)),
                  gt_spec, pl.BlockSpec((1, rows, LANES), lambda i: (0, 0, 0)), any_spec],
        out_specs=tok,
        out_shape=jax.ShapeDtypeStruct((n, rows, LANES), F32),
        scratch_shapes=[pltpu.SMEM((TOP_K, tm), jnp.int32), pltpu.SMEM((TOP_K, tm), F32),
                        pltpu.VMEM((tm * TOP_K, SLAB_ROWS, LANES), jnp.uint32),
                        pltpu.VMEM((tm, rows, LANES), F32),
                        pltpu.SemaphoreType.DMA((2,)), pltpu.SemaphoreType.DMA(())],
        compiler_params=_cparams(("arbitrary",), 40),
        name="final",
    )(_token_tiles(dest, tm), _token_tiles(gate, tm), x_slabs, shared_slabs, mod_slabs,
      final_norm.reshape(1, rows, LANES), yb)


def _project(x3, mod3, norm1, w_in, tm_norm, tm):
    b, t, d = x3.shape
    h = _norm_mod(x3, norm1, mod3, 1, 0, tm_norm).reshape(b * t, d)
    tn = 512
    q, k, v = (_matmul(h, w_in, i * ATTN_WIDTH, ATTN_WIDTH, tm, tn) for i in range(3))
    main = _matmul(h, w_in, ATTN_COLS, RWKV_MAIN, tm, tn)
    tail = _matmul(h, w_in[:, ATTN_COLS + RWKV_MAIN:], 0, LORA_WIDTH, tm, LORA_WIDTH)
    return q, k, v, main, tail


def kernel(x_prompt, x_sample, cache_attn_k, cache_attn_v, state_rwkv, state_shift, c_prompt, c_sample,
           rel_bias, norm1, norm2, final_norm, w_ada, b_ada, w_in, w_out, tmix_mu, w0, w2, a0, a2, g2,
           k_k, k_a, r_k, ln_w, ln_b, router_w, router_b, ws_gate, ws_up, ws_down, w_gate, w_up, w_down):
    bp, sp, d = x_prompt.shape
    bs = x_sample.shape[0]
    assert w_in.shape[0] == 1
    l = 0

    c_all = jnp.concatenate([c_prompt, c_sample, jnp.zeros((4, d), F32)], axis=0)
    mod = _ada_mod(c_all, w_ada[l], b_ada[l])
    mod_p = mod[:bp].reshape(bp, 1, 6 * d)
    mod_s = mod[bp:bp + bs].reshape(1, bs, 6 * d)
    xs3 = x_sample.reshape(1, bs, d)
    tm_p, tm_s, tm_norm = 512, bs, 256

    rwkv_par = (tmix_mu[l], w0[l], w2[l], a0[l], a2[l], g2[l], k_k[l], k_a[l], r_k[l], ln_w[l], ln_b[l])

    qp, kp, vp, main_p, tail_p = _project(x_prompt, mod_p, norm1[l], w_in[l], tm_norm, tm_p)
    heads = lambda a: a.reshape(bp, sp, H_ATTN, HEAD_DIM).transpose(0, 2, 1, 3).astype(BF16)
    o_attn_p = _prompt_attention(heads(qp), heads(kp), heads(vp), rel_bias)
    o_attn_p = o_attn_p.transpose(0, 2, 1, 3).reshape(bp, sp, ATTN_WIDTH)
    main_p3 = main_p.reshape(bp, sp, RWKV_MAIN)
    tail_p3 = tail_p.reshape(bp, sp, LORA_WIDTH)
    o_rwkv_p, state_p = _rwkv_time_mix(main_p3, tail_p3, jnp.zeros((bp, RWKV_COLS), F32),
                                       jnp.zeros((bp, H_RWKV, HEAD_DIM, HEAD_DIM), F32), *rwkv_par,
                                       tm=128, tt=32, single_step=False)
    shift_p = jnp.concatenate([main_p3[:, -1], tail_p3[:, -1]], axis=-1)
    x1_p = _matmul_gated_residual(jnp.concatenate([o_attn_p, o_rwkv_p], axis=-1), w_out[l], x_prompt, mod_p, 2, tm_p)
    router = (router_w[l], router_b[l])
    h2_p, e_p, g_p = _norm_mod(x1_p, norm2[l], mod_p, 4, 3, tm_norm, router)

    qs, ks, vs, main_s, tail_s = _project(xs3, mod_s, norm1[l], w_in[l], tm_s, tm_s)
    hd = lambda a: a.reshape(bs, H_ATTN, HEAD_DIM)
    pos_minor = lambda c: jnp.transpose(c[l], (0, 2, 3, 1))
    o_attn_s = _sample_attention(hd(qs), hd(ks), hd(vs), pos_minor(cache_attn_k), pos_minor(cache_attn_v),
                                 rel_bias).astype(BF16)
    o_rwkv_s, state_s = _rwkv_time_mix(main_s.reshape(1, bs, RWKV_MAIN), tail_s.reshape(1, bs, LORA_WIDTH),
                                       state_shift[l].reshape(1, bs, RWKV_COLS), state_rwkv[l], *rwkv_par,
                                       tm=bs, tt=1, single_step=True)
    shift_s = jnp.concatenate([main_s, tail_s], axis=-1)
    o_cat_s = jnp.concatenate([o_attn_s.reshape(1, bs, ATTN_WIDTH), o_rwkv_s], axis=-1)
    x1_s = _matmul_gated_residual(o_cat_s, w_out[l], xs3, mod_s, 2, tm_s)
    h2_s, e_s, g_s = _norm_mod(x1_s, norm2[l], mod_s, 4, 3, tm_s, router)

    n_p, n_s = bp * sp, bs
    n_all = n_p + n_s
    n_pad = -(-n_all // MOE_ROWS) * MOE_ROWS
    h_slabs = jnp.concatenate([h2_p, h2_s, jnp.zeros((n_pad - n_all, SLAB_ROWS, LANES), jnp.uint32)], axis=0)
    top = lambda a, b: jnp.concatenate([a.reshape(n_p, LANES)[:, :TOP_K], b.reshape(n_s, LANES)[:, :TOP_K]], axis=0)
    eidx, gate = top(e_p, e_s), top(g_p, g_s)
    n_blocks = -(-(n_all * TOP_K) // MOE_ROWS) + N_EXPERTS
    dest, blk_expert = _dispatch_plan(eidx, n_blocks)
    yb = _grouped_mlp(_dispatch(h_slabs, dest, n_blocks * MOE_ROWS), blk_expert, w_gate[l], w_up[l], w_down[l])
    shared = _grouped_mlp(h_slabs, jnp.zeros((n_pad // MOE_ROWS,), jnp.int32), ws_gate, ws_up, ws_down)

    rows = 2 * SLAB_ROWS
    tm_f = 128
    y_p = _final(x1_p.reshape(n_p, rows, LANES), shared, 0, yb, dest[:n_p], gate[:n_p],
                 mod_p.reshape(bp, 6 * rows, LANES), sp // tm_f, final_norm, tm_f)
    y_s = _final(x1_s.reshape(n_s, rows, LANES), shared, n_p // n_s, yb, dest[n_p:], gate[n_p:],
                 mod_s.reshape(bs, 6 * rows, LANES), 0, final_norm, n_s)
    y_p = y_p.reshape(bp, sp, d)

    return (y_p, y_s.reshape(bs, 1, d),
            kp.reshape(1, bp, sp, H_ATTN, HEAD_DIM), vp.reshape(1, bp, sp, H_ATTN, HEAD_DIM),
            state_p[None], shift_p[None],
            ks.reshape(1, bs, 1, H_ATTN, HEAD_DIM), vs.reshape(1, bs, 1, H_ATTN, HEAD_DIM),
            state_s[None], shift_s[None])
```
